```python
import jax
import jax.numpy as jnp
from jax import lax
import numpy as np

D_MODEL = 1024
BATCH = 16
SEQ = 256
DEPTH = 4
DEC_BATCH = 2
DEC_SEQ = 4096
PAST_LEN = 256

GRID_W = 64

A_HEADS = 4
A_HEAD = 64
A_W = A_HEADS * A_HEAD
A_DECAY_LORA = 64
A_ICLR_LORA = 64
A_GATE_LORA = 128
A_GN_EPS = 64e-5

B_HEADS = 4
B_DK = 64
B_DV = 64
B_W = B_HEADS * B_DV
B_CHUNK = 64

C_HEADS = 8
C_NOPE = 64
C_ROPE = 32
C_V = 64
C_Q_LORA = 256
C_KV_LORA = 128
C_W = C_HEADS * C_V
ROPE_BASE = 10000.0
ATTN_BLOCK = 128

N_EXPERTS = 32
TOP_K = 4
D_EXPERT = 1024
SWIGLU_LIMIT = 7.0
SWIGLU_ALPHA = 1.702
MOE_BLOCK = 256

A_IN = 3 * A_W + A_DECAY_LORA + A_ICLR_LORA + A_GATE_LORA
B_IN = 4 * B_W + 4 * B_HEADS
C_IN = C_Q_LORA + C_KV_LORA + C_ROPE
G_IN = 3 * D_MODEL
N_IN = A_IN + B_IN + C_IN + G_IN

DN_ALPHA = (2 * DEPTH) ** 0.25
DN_BETA = (8 * DEPTH) ** -0.25
LN_EPS = 1e-5
RMS_EPS = 1e-6

kernel_name = 'hybrid_rwkv7_mlstm_mla_moe_flow_step'


def _split(x, sizes):
    return jnp.split(x, np.cumsum(sizes)[:-1].tolist(), axis=-1)


def layer_norm(x, w, b, eps=LN_EPS):
    xf = x.astype(jnp.float32)
    mu = xf.mean(-1, keepdims=True)
    var = jnp.square(xf - mu).mean(-1, keepdims=True)
    return ((xf - mu) * lax.rsqrt(var + eps) * w.astype(jnp.float32) + b.astype(jnp.float32)).astype(x.dtype)


def rms_norm(x, w, eps=RMS_EPS):
    xf = x.astype(jnp.float32)
    y = xf * lax.rsqrt(jnp.square(xf).mean(-1, keepdims=True) + eps)
    return (y * w.astype(jnp.float32)).astype(x.dtype)


def head_norm(x, w, b, eps):
    h, n = x.shape[-2:]
    xf = x.astype(jnp.float32)
    mu = xf.mean(-1, keepdims=True)
    var = jnp.square(xf - mu).mean(-1, keepdims=True)
    y = (xf - mu) * lax.rsqrt(var + eps)
    return (y * w.reshape(h, n).astype(jnp.float32) + b.reshape(h, n).astype(jnp.float32)).astype(x.dtype)


def centred_shift(x):
    xp = jnp.pad(x, ((0, 0), (1, 1), (0, 0)))
    return 0.5 * (xp[:, :-2] + xp[:, 2:])


def centred_conv3(x, w):
    xp = jnp.pad(x, ((0, 0), (1, 1), (0, 0)))
    return w[0] * xp[:, :-2] + w[1] * xp[:, 1:-1] + w[2] * xp[:, 2:]


def axial_rope(n_tok, dtype):
    rows = n_tok // GRID_W
    r, col = jnp.meshgrid(jnp.arange(rows, dtype=jnp.float32), jnp.arange(GRID_W, dtype=jnp.float32), indexing='ij')
    n_freq = C_ROPE // 4
    inv = 1.0 / (ROPE_BASE ** (jnp.arange(n_freq, dtype=jnp.float32) / n_freq))
    ang = jnp.concatenate([r.reshape(-1, 1) * inv, col.reshape(-1, 1) * inv], axis=-1)
    return jnp.cos(ang).astype(dtype), jnp.sin(ang).astype(dtype)


def rotate(x, cos, sin):
    x1, x2 = jnp.split(x, 2, axis=-1)
    return jnp.concatenate([x1 * cos - x2 * sin, x2 * cos + x1 * sin], axis=-1)


def rwkv7_scan(r, w, k, v, kk, a, s0, reverse):
    xs = tuple(jnp.moveaxis(t.astype(jnp.float32), 1, 0) for t in (r, w, k, v, kk, a))

    def step(s, inp):
        r_t, w_t, k_t, v_t, kk_t, a_t = inp
        sa = jnp.einsum('bhij,bhj->bhi', s, -kk_t)
        s = (s * w_t[:, :, None, :] + sa[..., None] * (kk_t * a_t)[:, :, None, :]
             + v_t[..., None] * k_t[:, :, None, :])
        return s, jnp.einsum('bhij,bhj->bhi', s, r_t)

    s_fin, ys = lax.scan(step, s0.astype(jnp.float32), xs, reverse=reverse)
    return jnp.moveaxis(ys, 0, 1), s_fin


def rwkv7_branch(xa, p, s0):
    bsz, n_tok, _ = xa.shape
    xa = xa + p['rw_mu'] * (centred_shift(xa) - xa)
    r, k, v, wl, al, gl = _split(xa, [A_W, A_W, A_W, A_DECAY_LORA, A_ICLR_LORA, A_GATE_LORA])
    heads = lambda t: t.reshape(bsz, n_tok, A_HEADS, A_HEAD)
    g = jax.nn.sigmoid(gl) @ p['rw_g2']
    kk = heads(k * p['rw_kk'])
    kk = kk / jnp.maximum(jnp.linalg.norm(kk, axis=-1, keepdims=True), 1e-6)
    rh, vh, tw = heads(r), heads(v), jnp.tanh(wl)
    ys, bonus, finals = [], [], []
    for d in range(2):
        w_log = -jax.nn.softplus(-(p['rw_w0'][d] + tw @ p['rw_w2'][d]).astype(jnp.float32)) - 0.5
        decay = jnp.exp(-jnp.exp(w_log))
        a = jax.nn.sigmoid(p['rw_a0'][d] + al @ p['rw_a2'][d])
        kd = heads(k * (1 + (a - 1) * p['rw_ka']))
        y_d, s_d = rwkv7_scan(rh, heads(decay), kd, vh, kk, heads(a), s0[:, d], d == 1)
        ys.append(y_d)
        finals.append(s_d.astype(xa.dtype))
        bonus.append(jnp.sum(rh * kd * p['rw_rk'], axis=-1, keepdims=True) * vh)
    y = head_norm((ys[0] + ys[1]).astype(xa.dtype), p['rw_ln_w'], p['rw_ln_b'], A_GN_EPS)
    out = (y + bonus[0] + bonus[1]).reshape(bsz, n_tok, A_W) * g
    return out, jnp.stack(finals, axis=1)


def mlstm_chunkwise(q, k, v, log_i, log_f, c0, n0, m0):
    f32 = jnp.float32
    bsz, nh, n_tok, _ = q.shape
    nc = n_tok // B_CHUNK

    def chunks(t):
        t = t.astype(f32).reshape(bsz, nh, nc, B_CHUNK, *t.shape[3:])
        return jnp.moveaxis(t, 2, 0)

    causal = jnp.tril(jnp.ones((B_CHUNK, B_CHUNK), dtype=bool))

    def step(carry, inp):
        c, n, m = carry
        qc, kc, vc, lic, lfc = inp
        b = jnp.cumsum(lfc, axis=-1)
        dlog = jnp.where(causal, b[..., :, None] - b[..., None, :] + lic[..., None, :], -jnp.inf)
        inter = b + m[..., None]
        mj = jnp.maximum(dlog.max(-1), inter)
        s = jnp.einsum('bhjd,bhsd->bhjs', qc, kc) * jnp.exp(dlog - mj[..., None])
        e_inter = jnp.exp(inter - mj)
        num = jnp.einsum('bhjs,bhsv->bhjv', s, vc) + e_inter[..., None] * jnp.einsum('bhjd,bhdv->bhjv', qc, c)
        den = s.sum(-1) + e_inter * jnp.einsum('bhjd,bhd->bhj', qc, n)
        h = num / jnp.maximum(jnp.abs(den), jnp.exp(-mj))[..., None]
        b_end = b[..., -1]
        wlog = b_end[..., None] - b + lic
        m_new = jnp.maximum(b_end + m, wlog.max(-1))
        wk = jnp.exp(wlog - m_new[..., None])
        dec = jnp.exp(b_end + m - m_new)
        c_new = dec[..., None, None] * c + jnp.einsum('bhs,bhsd,bhsv->bhdv', wk, kc, vc)
        n_new = dec[..., None] * n + jnp.einsum('bhs,bhsd->bhd', wk, kc)
        return (c_new, n_new, m_new), h

    init = (c0.astype(f32), n0.astype(f32), m0.astype(f32))
    (c, n, m), hs = lax.scan(step, init, tuple(chunks(t) for t in (q, k, v, log_i, log_f)))
    h = jnp.moveaxis(hs, 0, 2).reshape(bsz, nh, n_tok, -1)
    return h, c, n, m


def mlstm_branch(xb, p, state0):
    bsz, n_tok, _ = xb.shape
    qk, v, o, gates = _split(xb, [2 * B_W, B_W, B_W, 4 * B_HEADS])
    q, k = jnp.split(jax.nn.silu(centred_conv3(qk, p['ml_conv'])), 2, axis=-1)
    heads = lambda t, dh: jnp.transpose(t.reshape(bsz, n_tok, B_HEADS, dh), (0, 2, 1, 3))
    q, k, v = heads(q, B_DK), heads(k, B_DK) * (B_DK ** -0.5), heads(v, B_DV)
    g = gates.reshape(bsz, n_tok, 4, B_HEADS).astype(jnp.float32) + p['ml_gate_b'].astype(jnp.float32)
    g = jnp.transpose(g, (2, 0, 3, 1))
    c0, n0, m0 = state0
    flip = lambda t: jnp.flip(t, axis=2)
    ident = lambda t: t
    hs, cs, ns, ms = [], [], [], []
    for d in range(2):
        order = ident if d == 0 else flip
        h, c_d, n_d, m_d = mlstm_chunkwise(order(q), order(k), order(v), order(g[2 * d]),
                                           jax.nn.log_sigmoid(order(g[2 * d + 1])),
                                           c0[:, d], n0[:, d], m0[:, d])
        hs.append(order(h))
        cs.append(c_d.astype(xb.dtype))
        ns.append(n_d.astype(xb.dtype))
        ms.append(m_d.astype(xb.dtype))
    h = jnp.transpose(hs[0] + hs[1], (0, 2, 1, 3)).astype(xb.dtype)
    h = head_norm(h, p['ml_ln_w'], p['ml_ln_b'], LN_EPS)
    out = h.reshape(bsz, n_tok, B_W) * jax.nn.sigmoid(o)
    return out, (jnp.stack(cs, axis=1), jnp.stack(ns, axis=1), jnp.stack(ms, axis=1))


def blocked_mla_attention(q_nope, q_rope, k_nope, k_rope, v):
    bsz, n_q, nh, _ = q_nope.shape
    nb = n_q // ATTN_BLOCK
    scale = (C_NOPE + C_ROPE) ** -0.5
    blocks = lambda t: jnp.moveaxis(t.reshape(bsz, nb, ATTN_BLOCK, *t.shape[2:]), 1, 0)

    def attend(qs):
        qn, qr = qs
        s = jnp.einsum('bqhd,bkhd->bhqk', qn, k_nope) + jnp.einsum('bqhd,bkd->bhqk', qr, k_rope)
        pr = jax.nn.softmax(s.astype(jnp.float32) * scale, axis=-1)
        return jnp.einsum('bhqk,bkhd->bqhd', pr.astype(v.dtype), v)

    o = lax.map(attend, (blocks(q_nope), blocks(q_rope)))
    return jnp.moveaxis(o, 0, 1).reshape(bsz, n_q, nh, -1)


def mla_branch(xc, p, rope, ctx_cache):
    bsz, n_tok, _ = xc.shape
    q_dn, kv_dn, k_pe = _split(xc, [C_Q_LORA, C_KV_LORA, C_ROPE])
    q = (rms_norm(q_dn, p['mla_q_norm']) @ p['mla_wuq']).reshape(bsz, n_tok, C_HEADS, C_NOPE + C_ROPE)
    q_nope, q_rope = q[..., :C_NOPE], q[..., C_NOPE:]
    ckv = rms_norm(kv_dn, p['mla_kv_norm'])
    if rope is not None:
        cos, sin = rope
        q_rope = rotate(q_rope, cos[None, :, None, :], sin[None, :, None, :])
        k_pe = rotate(k_pe, cos[None], sin[None])
    if ctx_cache is None:
        ckv_all, kpe_all = ckv, k_pe
    else:
        ckv_all = jnp.concatenate([ctx_cache[0].astype(ckv.dtype), ckv], axis=1)
        kpe_all = jnp.concatenate([ctx_cache[1].astype(k_pe.dtype), k_pe], axis=1)
    n_k = ckv_all.shape[1]
    k_nope = (ckv_all @ p['mla_wuk']).reshape(bsz, n_k, C_HEADS, C_NOPE)
    v = (ckv_all @ p['mla_wuv']).reshape(bsz, n_k, C_HEADS, C_V)
    o = blocked_mla_attention(q_nope, q_rope, k_nope, kpe_all, v)
    return o.reshape(bsz, n_tok, C_W), (ckv, k_pe)


def token_mixer(u, p, rope, ctx):
    bsz = u.shape[0]
    if ctx is None:
        z = lambda *s: jnp.zeros((bsz, 2) + s, jnp.float32)
        rw0 = z(A_HEADS, A_HEAD, A_HEAD)
        ml0 = (z(B_HEADS, B_DK, B_DV), z(B_HEADS, B_DK), z(B_HEADS))
        mla_ctx = None
    else:
        rw0, ml0, mla_ctx = ctx[0], (ctx[1], ctx[2], ctx[3]), (ctx[4], ctx[5])
    proj = u @ p['w_in']
    xa, xb, xc, xg = _split(proj, [A_IN, B_IN, C_IN, G_IN])
    ya, s_rwkv = rwkv7_branch(xa, p, rw0)
    yb, (s_c, s_n, s_m) = mlstm_branch(xb, p, ml0)
    yc, (ckv, kpe) = mla_branch(xc, p, rope, mla_ctx)
    g_a, g_b, g_c = jnp.split(jax.nn.sigmoid(xg), 3, axis=-1)
    merged = g_a * (ya @ p['proj_a']) + g_b * (yb @ p['proj_b']) + g_c * (yc @ p['proj_c'])
    return merged @ p['w_out'], (s_rwkv, s_c, s_n, s_m, ckv, kpe)


def moe_ffn(u, p):
    bsz, n_tok, dm = u.shape
    xf = u.reshape(-1, dm)
    n_rows = xf.shape[0]
    logits = (xf @ p['router_w'] + p['router_b']).astype(jnp.float32)
    top_v, top_e = lax.top_k(logits, TOP_K)
    top_w = jax.nn.softmax(top_v, axis=-1)
    n_assign = n_rows * TOP_K
    flat_e = top_e.reshape(-1)
    flat_tok = jnp.repeat(jnp.arange(n_rows, dtype=jnp.int32), TOP_K)
    order = jnp.argsort(flat_e)
    e_sorted = flat_e[order]
    counts = jnp.bincount(flat_e, length=N_EXPERTS)
    padded = (counts + MOE_BLOCK - 1) // MOE_BLOCK * MOE_BLOCK
    pad_end = jnp.cumsum(padded)
    pad_start = pad_end - padded
    start = jnp.cumsum(counts) - counts
    dest = pad_start[e_sorted] + jnp.arange(n_assign, dtype=jnp.int32) - start[e_sorted]
    n_blocks = -(-n_assign // MOE_BLOCK) + N_EXPERTS
    n_pad = n_blocks * MOE_BLOCK
    row_tok = jnp.full((n_pad,), n_rows, jnp.int32).at[dest].set(flat_tok[order])
    row_w = jnp.zeros((n_pad,), jnp.float32).at[dest].set(top_w.reshape(-1)[order])
    block_start = jnp.arange(n_blocks, dtype=jnp.int32) * MOE_BLOCK
    block_e = jnp.minimum(jnp.sum(pad_end[None, :] <= block_start[:, None], axis=-1), N_EXPERTS - 1)
    x_pad = jnp.concatenate([xf, jnp.zeros((1, dm), xf.dtype)], axis=0)

    def expert_block(args):
        toks, e = args
        h = x_pad[toks] @ p['moe_w1'][e] + p['moe_b1'][e]
        h_glu = jnp.minimum(h[:, 0::2], SWIGLU_LIMIT)
        h_lin = jnp.clip(h[:, 1::2], -SWIGLU_LIMIT, SWIGLU_LIMIT)
        act = h_glu * jax.nn.sigmoid(SWIGLU_ALPHA * h_glu) * (h_lin + 1)
        return act @ p['moe_w2'][e] + p['moe_b2'][e]

    y_rows = lax.map(expert_block, (row_tok.reshape(n_blocks, MOE_BLOCK), block_e))
    y_rows = y_rows.reshape(n_pad, dm) * row_w[:, None].astype(xf.dtype)
    y = jax.ops.segment_sum(y_rows, row_tok, num_segments=n_rows + 1)[:n_rows]
    return y.reshape(bsz, n_tok, dm)


def trunk_layer(x, cond, p, rope, ctx):
    mod = (jax.nn.silu(cond) @ p['ada_w'] + p['ada_b'])[..., None, :]
    sh1, sc1, g1, sh2, sc2, g2 = jnp.split(mod, 6, axis=-1)
    mix, ctx_out = token_mixer(x * (1 + sc1) + sh1, p, rope, ctx)
    x = layer_norm(DN_ALPHA * x + g1 * mix, p['ln1_w'], p['ln1_b'])
    ffn = moe_ffn(x * (1 + sc2) + sh2, p)
    x = layer_norm(DN_ALPHA * x + g2 * ffn, p['ln2_w'], p['ln2_b'])
    return x, ctx_out


def setup_inputs(seed: int = 0) -> dict:
    key = jax.random.key(seed)
    ks = iter(jax.random.split(key, 64))
    nrm = lambda shape, scale: scale * jax.random.normal(next(ks), shape, jnp.float32)
    unif = lambda shape, lo, hi: jax.random.uniform(next(ks), shape, jnp.float32, lo, hi)
    L = DEPTH
    d = D_MODEL
    fbias = jnp.linspace(3.0, 6.0, B_HEADS)
    return {
        'x_prompt': nrm((BATCH, SEQ, d), 1.0),
        'x_sample': nrm((DEC_BATCH, DEC_SEQ, d), 1.0),
        'state_rwkv': nrm((DEC_BATCH, L, 2, A_HEADS, A_HEAD, A_HEAD), 0.5),
        'state_mlstm_c': nrm((DEC_BATCH, L, 2, B_HEADS, B_DK, B_DV), 0.5),
        'state_mlstm_n': nrm((DEC_BATCH, L, 2, B_HEADS, B_DK), 0.5),
        'state_mlstm_m': nrm((DEC_BATCH, L, 2, B_HEADS), 1.0),
        'cache_mla_ckv': nrm((DEC_BATCH, L, PAST_LEN, C_KV_LORA), 1.0),
        'cache_mla_kpe': nrm((DEC_BATCH, L, PAST_LEN, C_ROPE), 1.0),
        'c': nrm((DEC_BATCH, d), 1.0),
        'c_ctx': nrm((d,), 1.0),
        'ada_w': nrm((L, d, 6 * d), d ** -0.5),
        'ada_b': nrm((L, 6 * d), 0.02),
        'w_in': nrm((L, d, N_IN), d ** -0.5),
        'rw_mu': unif((L, A_IN), 0.0, 1.0),
        'rw_w0': unif((L, 2, A_W), -6.0, 0.0),
        'rw_w2': nrm((L, 2, A_DECAY_LORA, A_W), 0.1),
        'rw_a0': nrm((L, 2, A_W), 0.1),
        'rw_a2': nrm((L, 2, A_ICLR_LORA, A_W), 0.1),
        'rw_g2': nrm((L, A_GATE_LORA, A_W), A_GATE_LORA ** -0.5),
        'rw_kk': 0.85 + nrm((L, A_W), 0.02),
        'rw_ka': 1.0 + nrm((L, A_W), 0.02),
        'rw_rk': nrm((L, A_HEADS, A_HEAD), 0.1),
        'rw_ln_w': 1.0 + nrm((L, A_W), 0.02),
        'rw_ln_b': nrm((L, A_W), 0.02),
        'ml_conv': nrm((L, 3, 2 * B_W), 3 ** -0.5),
        'ml_gate_b': jnp.stack([nrm((L, B_HEADS), 0.1), fbias + nrm((L, B_HEADS), 0.1),
                                nrm((L, B_HEADS), 0.1), fbias + nrm((L, B_HEADS), 0.1)], axis=1),
        'ml_ln_w': 1.0 + nrm((L, B_W), 0.02),
        'ml_ln_b': nrm((L, B_W), 0.02),
        'mla_q_norm': 1.0 + nrm((L, C_Q_LORA), 0.02),
        'mla_wuq': nrm((L, C_Q_LORA, C_HEADS * (C_NOPE + C_ROPE)), C_Q_LORA ** -0.5),
        'mla_kv_norm': 1.0 + nrm((L, C_KV_LORA), 0.02),
        'mla_wuk': nrm((L, C_KV_LORA, C_HEADS * C_NOPE), C_KV_LORA ** -0.5),
        'mla_wuv': nrm((L, C_KV_LORA, C_W), C_KV_LORA ** -0.5),
        'proj_a': nrm((L, A_W, d), A_W ** -0.5),
        'proj_b': nrm((L, B_W, d), B_W ** -0.5),
        'proj_c': nrm((L, C_W, d), C_W ** -0.5),
        'w_out': nrm((L, d, d), DN_BETA * d ** -0.5),
        'ln1_w': 1.0 + nrm((L, d), 0.02),
        'ln1_b': nrm((L, d), 0.02),
        'router_w': nrm((L, d, N_EXPERTS), d ** -0.5),
        'router_b': nrm((L, N_EXPERTS), 0.01),
        'moe_w1': nrm((L, N_EXPERTS, d, 2 * D_EXPERT), d ** -0.5),
        'moe_b1': nrm((L, N_EXPERTS, 2 * D_EXPERT), 0.02),
        'moe_w2': nrm((L, N_EXPERTS, D_EXPERT, d), DN_BETA * D_EXPERT ** -0.5),
        'moe_b2': nrm((L, N_EXPERTS, d), 0.02),
        'ln2_w': 1.0 + nrm((L, d), 0.02),
        'ln2_b': nrm((L, d), 0.02),
    }


def reference(x_prompt, x_sample, state_rwkv, state_mlstm_c, state_mlstm_n, state_mlstm_m,
              cache_mla_ckv, cache_mla_kpe, c, c_ctx, ada_w, ada_b, w_in, rw_mu, rw_w0, rw_w2,
              rw_a0, rw_a2, rw_g2, rw_kk, rw_ka, rw_rk, rw_ln_w, rw_ln_b, ml_conv, ml_gate_b,
              ml_ln_w, ml_ln_b, mla_q_norm, mla_wuq, mla_kv_norm, mla_wuk, mla_wuv, proj_a, proj_b,
              proj_c, w_out, ln1_w, ln1_b, router_w, router_b, moe_w1, moe_b1, moe_w2, moe_b2,
              ln2_w, ln2_b):
    rope = axial_rope(x_sample.shape[1], x_sample.dtype)
    y_prompt, y_sample = x_prompt, x_sample
    ctx_layers = []
    for l in range(DEPTH):
        p = {'ada_w': ada_w[l], 'ada_b': ada_b[l], 'w_in': w_in[l], 'rw_mu': rw_mu[l],
             'rw_w0': rw_w0[l], 'rw_w2': rw_w2[l], 'rw_a0': rw_a0[l], 'rw_a2': rw_a2[l],
             'rw_g2': rw_g2[l], 'rw_kk': rw_kk[l], 'rw_ka': rw_ka[l], 'rw_rk': rw_rk[l],
             'rw_ln_w': rw_ln_w[l], 'rw_ln_b': rw_ln_b[l], 'ml_conv': ml_conv[l],
             'ml_gate_b': ml_gate_b[l], 'ml_ln_w': ml_ln_w[l], 'ml_ln_b': ml_ln_b[l],
             'mla_q_norm': mla_q_norm[l], 'mla_wuq': mla_wuq[l], 'mla_kv_norm': mla_kv_norm[l],
             'mla_wuk': mla_wuk[l], 'mla_wuv': mla_wuv[l], 'proj_a': proj_a[l], 'proj_b': proj_b[l],
             'proj_c': proj_c[l], 'w_out': w_out[l], 'ln1_w': ln1_w[l], 'ln1_b': ln1_b[l],
             'router_w': router_w[l], 'router_b': router_b[l], 'moe_w1': moe_w1[l],
             'moe_b1': moe_b1[l], 'moe_w2': moe_w2[l], 'moe_b2': moe_b2[l],
             'ln2_w': ln2_w[l], 'ln2_b': ln2_b[l]}
        y_prompt, ctx_new = trunk_layer(y_prompt, c_ctx, p, None, None)
        ctx_layers.append(ctx_new)
        cached = (state_rwkv[:, l], state_mlstm_c[:, l], state_mlstm_n[:, l], state_mlstm_m[:, l],
                  cache_mla_ckv[:, l], cache_mla_kpe[:, l])
        y_sample, _ = trunk_layer(y_sample, c, p, rope, cached)
    stack = lambda i: jnp.stack([s[i] for s in ctx_layers], axis=1)
    return (y_prompt, y_sample, stack(0), stack(1), stack(2), stack(3), stack(4), stack(5))
```

```python
import functools

import numpy as np
import jax
import jax.numpy as jnp
from jax import lax
from jax.experimental import pallas as pl
from jax.experimental.pallas import tpu as pltpu

F32 = jnp.float32
BF16 = jnp.bfloat16

D_MODEL = 1024
BATCH = 16
SEQ = 256
DEPTH = 4
DEC_BATCH = 2
DEC_SEQ = 4096
PAST_LEN = 256
GRID_W = 64

A_HEADS = 4
A_HEAD = 64
A_W = 256
A_GN_EPS = 64e-5
B_HEADS = 4
B_DK = 64
B_W = 256
C_HEADS = 8
C_NOPE = 64
C_ROPE = 32
C_V = 64
C_Q_LORA = 256
C_KV_LORA = 128
C_W = 512
ROPE_BASE = 10000.0
N_EXPERTS = 32
TOP_K = 4
D_EXPERT = 1024
SWIGLU_LIMIT = 7.0
SWIGLU_ALPHA = 1.702
MOE_BLOCK = 256
A_IN = 1024
B_IN = 1040
C_IN = 416
DN_ALPHA = (2 * DEPTH) ** 0.25
LN_EPS = 1e-5
RMS_EPS = 1e-6

TILE = 256
CHUNK = 64
N_TOK = BATCH * SEQ + DEC_BATCH * DEC_SEQ
N_TILES = N_TOK // TILE
N_PROMPT_TILES = BATCH * SEQ // TILE
TILES_PER_SAMPLE = DEC_SEQ // TILE
N_SEQ = BATCH + DEC_BATCH
HALO = 8

COL_A = 0
COL_G = 1024
COL_B = 4096
COL_C = 5120
COL_BG = 5632
N_PROJ = 5888

VMEM_LIMIT = 56 * 1024 * 1024


def _cparams(sem):
    return pltpu.CompilerParams(dimension_semantics=sem, vmem_limit_bytes=VMEM_LIMIT)


NN = ((1,), (0,))
NT = ((1,), (1,))
TN = ((0,), (0,))


def _dot(a, b, dims=NN):
    return lax.dot_general(a, b, (dims, ((), ())), preferred_element_type=F32)


def _split(x, n):
    parts, r = [], x
    for i in range(n):
        p = r.astype(BF16)
        parts.append(p)
        if i + 1 < n:
            r = r - p.astype(F32)
    return parts


def _mm(a, b, passes=1, dims=NN):
    if passes == 1:
        return _dot(a.astype(BF16), b.astype(BF16), dims)
    ah, al = _split(a, 2)
    bh, bl = _split(b, 2)
    return _dot(ah, bh, dims) + (_dot(al, bh, dims) + _dot(ah, bl, dims))


def _mm_exact_l(t01, x):
    x0, x1, x2 = _split(x, 3)
    return _dot(t01, x0) + (_dot(t01, x1) + _dot(t01, x2))


def _mm_exact_r(x, t01):
    x0, x1, x2 = _split(x, 3)
    return _dot(x0, t01) + (_dot(x1, t01) + _dot(x2, t01))


def _sigmoid(x):
    return 1.0 / (1.0 + jnp.exp(-x))


def _softplus(x):
    return jnp.maximum(x, 0.0) + jnp.log(1.0 + jnp.exp(-jnp.abs(x)))


def _block_ones(n, blk):
    r = lax.broadcasted_iota(jnp.int32, (n, n), 0) // blk
    c = lax.broadcasted_iota(jnp.int32, (n, n), 1) // blk
    return jnp.where(r == c, 1.0, 0.0).astype(BF16)


def _layer_norm(x, w, b):
    mu = jnp.mean(x, axis=-1, keepdims=True)
    xc = x - mu
    var = jnp.mean(xc * xc, axis=-1, keepdims=True)
    return xc * lax.rsqrt(var + LN_EPS) * w + b


def _head_norm(x, ones_blk, w, b, eps):
    mu = _mm_exact_r(x, ones_blk) * (1.0 / 64.0)
    xc = x - mu
    var = _mm_exact_r(xc * xc, ones_blk) * (1.0 / 64.0)
    return xc * lax.rsqrt(var + eps) * w + b


def _shifted(x, prev_row, next_row):
    rid = lax.broadcasted_iota(jnp.int32, x.shape, 0)
    xp = jnp.where(rid == 0, prev_row, pltpu.roll(x, 1, axis=0))
    xn = jnp.where(rid == x.shape[0] - 1, next_row, pltpu.roll(x, x.shape[0] - 1, axis=0))
    return xp, xn


def _schedule():
    seq_of_tile = np.concatenate([np.arange(BATCH), BATCH + np.repeat(np.arange(DEC_BATCH), TILES_PER_SAMPLE)])
    first_of_tile = np.ones(N_TILES, np.int32)
    last_of_tile = np.ones(N_TILES, np.int32)
    for s in range(DEC_BATCH):
        base = N_PROMPT_TILES + s * TILES_PER_SAMPLE
        first_of_tile[base + 1: base + TILES_PER_SAMPLE] = 0
        last_of_tile[base: base + TILES_PER_SAMPLE - 1] = 0
    tile_fwd = np.arange(N_TILES)
    tile_bwd = np.arange(N_TILES)
    for s in range(DEC_BATCH):
        base = N_PROMPT_TILES + s * TILES_PER_SAMPLE
        tile_bwd[base: base + TILES_PER_SAMPLE] = base + TILES_PER_SAMPLE - 1 - np.arange(TILES_PER_SAMPLE)
    tile = np.concatenate([tile_fwd, tile_bwd]).astype(np.int32)
    seq = seq_of_tile[tile].astype(np.int32)
    begins = np.concatenate([first_of_tile[tile_fwd], last_of_tile[tile_bwd]]).astype(np.int32)
    ends = np.concatenate([last_of_tile[tile_fwd], first_of_tile[tile_bwd]]).astype(np.int32)
    has_prev = (1 - first_of_tile).astype(np.int32)
    has_next = (1 - last_of_tile).astype(np.int32)
    return tile, seq, begins, ends, has_prev, has_next


_SCHED = _schedule()
_SEQ_OF_TILE = np.concatenate([np.arange(BATCH), BATCH + np.repeat(np.arange(DEC_BATCH), TILES_PER_SAMPLE)]).astype(np.int32)
_MOD_ROW_OF_TILE = np.concatenate([np.zeros(N_PROMPT_TILES), 1 + np.repeat(np.arange(DEC_BATCH), TILES_PER_SAMPLE)]).astype(np.int32)


def _direction_masks(d):
    row = lax.broadcasted_iota(jnp.int32, (CHUNK, CHUNK), 0)
    col = lax.broadcasted_iota(jnp.int32, (CHUNK, CHUNK), 1)
    diff = (row - col) * (1 - 2 * d)
    return diff >= 0, diff > 0, diff <= 0


def _mod_kernel(c_ref, w_ref, b_ref, o_ref):
    c = c_ref[...]
    o_ref[0] = _mm(c * _sigmoid(c), w_ref[0], 3) + b_ref[0]


def _modulation(cond, ada_w, ada_b):
    tn = 1536
    return pl.pallas_call(
        _mod_kernel,
        grid=(DEPTH, 6 * D_MODEL // tn),
        in_specs=[pl.BlockSpec((8, D_MODEL), lambda l, j: (0, 0)),
                  pl.BlockSpec((1, D_MODEL, tn), lambda l, j: (l, 0, j)),
                  pl.BlockSpec((1, 1, tn), lambda l, j: (l, 0, j))],
        out_specs=pl.BlockSpec((1, 8, tn), lambda l, j: (l, 0, j)),
        out_shape=jax.ShapeDtypeStruct((DEPTH, 8, 6 * D_MODEL), F32),
        compiler_params=_cparams(("parallel", "parallel")),
        name="ada_modulation",
    )(cond, ada_w, ada_b.reshape(DEPTH, 1, 6 * D_MODEL))


IN_TM = 512
IN_TN = N_PROJ // 2


def _inproj_kernel(x_ref, sh_ref, sc_ref, w_ref, o_ref):
    u = x_ref[...] * (1.0 + sc_ref[0]) + sh_ref[0]
    o_ref[...] = _dot(u.astype(BF16), w_ref[...])


def _in_projection(x, modt, w):
    rep = IN_TM // TILE
    return pl.pallas_call(
        _inproj_kernel,
        grid=(N_PROJ // IN_TN, N_TOK // IN_TM),
        in_specs=[pl.BlockSpec((IN_TM, D_MODEL), lambda j, i: (i, 0)),
                  pl.BlockSpec((1, 1, D_MODEL), lambda j, i: (rep * i, 0, 0)),
                  pl.BlockSpec((1, 1, D_MODEL), lambda j, i: (rep * i, 0, 1)),
                  pl.BlockSpec((D_MODEL, IN_TN), lambda j, i: (0, j))],
        out_specs=pl.BlockSpec((IN_TM, IN_TN), lambda j, i: (i, j)),
        out_shape=jax.ShapeDtypeStruct((N_TOK, N_PROJ), F32),
        compiler_params=_cparams(("parallel", "parallel")),
        name="in_projection",
    )(x, modt, modt, w)


RW_P = 3


def _solve_unit_triangular(l_b, rhs, d):
    nb = CHUNK // 8
    xb = [rhs[8 * i:8 * i + 8, :] for i in range(nb)]
    lb = [l_b[8 * i:8 * i + 8, :] for i in range(nb)]
    order = range(CHUNK) if d == 0 else range(CHUNK - 1, -1, -1)
    for t in order:
        bt = t // 8
        row = xb[bt][t % 8:t % 8 + 1, :]
        blocks = range(bt, nb) if d == 0 else range(0, bt + 1)
        for i in blocks:
            xb[i] = xb[i] - lb[i][:, t:t + 1] * row
    return jnp.concatenate(xb, axis=0)


def _rwkv_kernel(d, tile_s, seq_s, begin_s, end_s, hasprev_s, hasnext_s,
                 xa_ref, xp_ref, xn_ref, s0_ref, mu_ref, w0_ref, w2_ref, a0_ref, a2_ref, g2_ref,
                 kk_ref, ka_ref, rk_ref,
                 out_ref, sfin_ref,
                 st_scr, r_scr, lw_scr, kd_scr, v_scr, kap_scr, b_scr):
    step = pl.program_id(0)
    tile = tile_s[step]

    @pl.when(begin_s[step] == 1)
    def _():
        st_scr[...] = s0_ref[0]

    x = xa_ref[...]
    prev_row = xp_ref[HALO - 1:HALO, :] * hasprev_s[tile].astype(F32)
    next_row = xn_ref[0:1, :] * hasnext_s[tile].astype(F32)
    xp, xn = _shifted(x, prev_row, next_row)
    xs = x + mu_ref[...] * (0.5 * (xp + xn) - x)
    r = xs[:, 0:256]
    k = xs[:, 256:512]
    v = xs[:, 512:768]
    wl = xs[:, 768:832]
    al = xs[:, 832:896]
    gl = xs[:, 896:1024]

    ones_blk = _block_ones(A_W, A_HEAD)
    kkv = k * kk_ref[...]
    nrm = jnp.sqrt(_mm_exact_r(kkv * kkv, ones_blk))
    kap = kkv / jnp.maximum(nrm, 1e-6)
    wpre = w0_ref[...] + _mm(jnp.tanh(wl), w2_ref[...], 3)
    lw = -jnp.exp(-_softplus(-wpre) - 0.5)
    a = _sigmoid(a0_ref[...] + _mm(al, a2_ref[...], 3))
    kd = k * (1.0 + (a - 1.0) * ka_ref[...])
    bonus = _mm_exact_r(r * kd * rk_ref[...], ones_blk) * v
    out_ref[:, 256:512] = bonus
    out_ref[:, 512:768] = _mm(_sigmoid(gl), g2_ref[...], 3)

    r_scr[...] = r
    lw_scr[...] = lw
    kd_scr[...] = kd
    v_scr[...] = v
    kap_scr[...] = kap
    b_scr[...] = kap * a

    incl, strict, _ = _direction_masks(d)
    tinc = jnp.where(incl, 1.0, 0.0).astype(BF16)
    eye = jnp.where(lax.broadcasted_iota(jnp.int32, (CHUNK, CHUNK), 0)
                    == lax.broadcasted_iota(jnp.int32, (CHUNK, CHUNK), 1), 1.0, 0.0)

    def chunk_body(c, carry):
        ce = c if d == 0 else TILE // CHUNK - 1 - c
        off = pl.multiple_of(ce * CHUNK, CHUNK)
        rows = pl.ds(off, CHUNK)
        lwc = lw_scr[rows, :]
        cs = _mm_exact_l(tinc, lwc)
        w_in = jnp.exp(cs)
        w_ex = jnp.exp(cs - lwc)
        w_inv = jnp.exp(-cs)
        w_tot = jnp.exp(jnp.sum(lwc, axis=0, keepdims=True))
        kt = kap_scr[rows, :] * w_ex
        rt = r_scr[rows, :] * w_in
        bt = b_scr[rows, :] * w_inv
        kdt = kd_scr[rows, :] * w_inv
        bh = bt * w_tot
        kh = kdt * w_tot
        vc = v_scr[rows, :]
        ys = []
        for h in range(A_HEADS):
            sl = slice(h * A_HEAD, (h + 1) * A_HEAD)
            gram = _mm(jnp.concatenate([kt[:, sl], rt[:, sl]], axis=0),
                       jnp.concatenate([bt[:, sl], kdt[:, sl]], axis=0), RW_P, NT)
            l_b = jnp.where(strict, gram[:CHUNK, :CHUNK], 0.0)
            l_k = jnp.where(strict, gram[:CHUNK, CHUNK:], 0.0)
            m_b = jnp.where(incl, gram[CHUNK:, :CHUNK], 0.0)
            m_k = jnp.where(incl, gram[CHUNK:, CHUNK:], 0.0)
            vh = vc[:, sl]
            xx = _solve_unit_triangular(l_b, jnp.concatenate([kt[:, sl], _mm(l_k, vh, RW_P)], axis=1), d)
            mx = _mm(m_b, xx, RW_P)
            rhat = rt[:, sl] - mx[:, :CHUNK]
            y0 = _mm(m_k, vh, RW_P) - mx[:, CHUNK:]
            bx = _mm(bh[:, sl], xx, RW_P, TN)
            g_mat = eye * w_tot[:, sl] - bx[:, :CHUNK]
            h_mat = _mm(kh[:, sl], vh, RW_P, TN) - bx[:, CHUNK:]
            st = st_scr[h]
            ys.append(_mm(rhat, st, RW_P) + y0)
            st_scr[h] = _mm(g_mat, st, RW_P) + h_mat
        out_ref[rows, 0:256] = jnp.concatenate(ys, axis=1)
        return carry

    lax.fori_loop(0, TILE // CHUNK, chunk_body, 0)

    @pl.when(end_s[step] == 1)
    def _():
        sfin_ref[0] = st_scr[...]


def _rwkv_direction(d, proj, s0, mu, w0, w2, a0, a2, g2, kk, ka, rk):
    nrb = N_TOK // HALO
    per = TILE // HALO
    tile, seq, begins, ends, has_prev, has_next = _SCHED
    half = slice(d * N_TILES, (d + 1) * N_TILES)
    sched = tuple(jnp.asarray(a) for a in (tile[half], seq[half], begins[half], ends[half], has_prev, has_next))

    def const(shape):
        return pl.BlockSpec(shape, lambda i, *_: (0,) * len(shape))

    grid_spec = pltpu.PrefetchScalarGridSpec(
        num_scalar_prefetch=6,
        grid=(N_TILES,),
        in_specs=[
            pl.BlockSpec((TILE, A_IN), lambda i, t, *_: (t[i], COL_A // A_IN)),
            pl.BlockSpec((HALO, A_IN), lambda i, t, *_: (jnp.maximum(t[i] * per - 1, 0), 0)),
            pl.BlockSpec((HALO, A_IN), lambda i, t, *_: (jnp.minimum(t[i] * per + per, nrb - 1), 0)),
            pl.BlockSpec((1, A_HEADS, A_HEAD, A_HEAD), lambda i, t, s, *_: (s[i], 0, 0, 0)),
            const((1, A_IN)),
            const((1, A_W)), const((64, A_W)), const((1, A_W)), const((64, A_W)),
            const((128, A_W)), const((1, A_W)), const((1, A_W)), const((1, A_W)),
        ],
        out_specs=[
            pl.BlockSpec((TILE, 768), lambda i, t, *_: (t[i], 0)),
            pl.BlockSpec((1, A_HEADS, A_HEAD, A_HEAD), lambda i, t, s, *_: (s[i], 0, 0, 0)),
        ],
        scratch_shapes=[pltpu.VMEM((A_HEADS, A_HEAD, A_HEAD), F32)] + [pltpu.VMEM((TILE, A_W), F32)] * 6,
    )
    return pl.pallas_call(
        functools.partial(_rwkv_kernel, d),
        grid_spec=grid_spec,
        out_shape=[jax.ShapeDtypeStruct((N_TOK, 768), F32),
                   jax.ShapeDtypeStruct((N_SEQ, A_HEADS, A_HEAD, A_HEAD), F32)],
        compiler_params=_cparams(("arbitrary",)),
        name="rwkv7_mixer_fwd" if d == 0 else "rwkv7_mixer_bwd",
    )(*sched, proj, proj, proj, s0, mu, w0, w2, a0, a2, g2, kk, ka, rk)


def _rwkv(proj, s0, mu, w0, w2, a0, a2, g2, kk, ka, rk):
    outs = [_rwkv_direction(d, proj, s0[:, d], mu, w0[d], w2[d], a0[d], a2[d], g2, kk, ka, rk) for d in range(2)]
    return (outs[0][0], outs[1][0]), (outs[0][1], outs[1][1])


ML_P = 3
N_COL = B_DK
M_COL = B_DK + 1
NEG = -1e30


def _mlstm_kernel(tile_s, seq_s, begin_s, end_s, hasprev_s, hasnext_s,
                  xb_ref, xp_ref, xn_ref, gt_ref, s0_ref, cw_ref, gb_ref,
                  out_ref, sfin_ref,
                  cn_scr, m_scr, q_scr, k_scr, v_scr, li_scr, lf_scr):
    d = pl.program_id(0)
    step = d * N_TILES + pl.program_id(1)
    tile = tile_s[step]

    @pl.when(begin_s[step] == 1)
    def _():
        s0 = s0_ref[0, 0]
        cn_scr[...] = s0
        for h in range(B_HEADS):
            m_scr[h:h + 1, :] = jnp.broadcast_to(s0[h, 0:1, M_COL:M_COL + 1], (1, 128))

    qk = xb_ref[:, 0:512]
    prev_row = xp_ref[HALO - 1:HALO, 0:512] * hasprev_s[tile].astype(F32)
    next_row = xn_ref[0:1, 0:512] * hasnext_s[tile].astype(F32)
    qp, qn = _shifted(qk, prev_row, next_row)
    conv = cw_ref[0:1, :] * qp + cw_ref[1:2, :] * qk + cw_ref[2:3, :] * qn
    act = conv * _sigmoid(conv)
    q_scr[...] = act[:, 0:256]
    k_scr[...] = act[:, 256:512] * (B_DK ** -0.5)
    v_scr[...] = xb_ref[:, 512:768]
    g = gt_ref[...] + gb_ref[0]
    li_scr[...] = g
    lf_scr[...] = -_softplus(-g)

    incl, _, incl_t = _direction_masks(d)
    tinc = jnp.where(incl, 1.0, 0.0).astype(BF16)
    tinc_t = jnp.where(incl_t, 1.0, 0.0).astype(BF16)
    lane = lax.broadcasted_iota(jnp.int32, (CHUNK, 128), 1)

    def chunk_body(c, carry):
        ce = c + d * (TILE // CHUNK - 1 - 2 * c)
        off = pl.multiple_of(ce * CHUNK, CHUNK)
        rows = pl.ds(off, CHUNK)
        li_c = li_scr[rows, :]
        lf_c = lf_scr[rows, :]
        li_r = li_c.T
        lf_r = lf_c.T
        b_c = _mm_exact_l(tinc, lf_c)
        b_r = _mm_exact_r(lf_r, tinc_t)
        b_end = jnp.sum(lf_c, axis=0, keepdims=True)
        qc = q_scr[rows, :]
        kc = k_scr[rows, :]
        vc = v_scr[rows, :]
        hs = []
        for h in range(B_HEADS):
            sl = slice(h * B_DK, (h + 1) * B_DK)
            fi = B_HEADS + h
            bc = b_c[:, fi:fi + 1]
            br = b_r[fi:fi + 1, :]
            lir = li_r[h:h + 1, :]
            lic = li_c[:, h:h + 1]
            m_old = m_scr[h:h + 1, 0:1]
            dlog = jnp.where(incl, bc - br + lir, NEG)
            inter = bc + m_old
            mj = jnp.maximum(jnp.max(dlog, axis=1, keepdims=True), inter)
            s = _mm(qc[:, sl], kc[:, sl], ML_P, NT) * jnp.exp(dlog - mj)
            e_int = jnp.exp(inter - mj)
            v_aug = jnp.where(lane == N_COL, 1.0,
                              jnp.concatenate([vc[:, sl], jnp.zeros((CHUNK, 128 - B_DK), F32)], axis=1))
            cn = cn_scr[h]
            nd = _mm(s, v_aug, ML_P) + e_int * _mm(qc[:, sl], cn, ML_P)
            den = jnp.maximum(jnp.abs(nd[:, N_COL:N_COL + 1]), jnp.exp(-mj))
            hs.append(nd[:, 0:B_DK] / den)
            be = b_end[:, fi:fi + 1]
            wlog = be - bc + lic
            m_new = jnp.maximum(be + m_old, jnp.max(wlog, axis=0, keepdims=True))
            wk = jnp.exp(wlog - m_new)
            dec = jnp.exp(be + m_old - m_new)
            cn_scr[h] = dec * cn + _mm(kc[:, sl], wk * v_aug, ML_P, TN)
            m_scr[h:h + 1, :] = jnp.broadcast_to(m_new, (1, 128))
        out_ref[0, rows, :] = jnp.concatenate(hs, axis=1)
        return carry

    lax.fori_loop(0, TILE // CHUNK, chunk_body, 0)

    @pl.when(end_s[step] == 1)
    def _():
        for h in range(B_HEADS):
            sfin_ref[0, 0, h] = jnp.where(lane == M_COL, m_scr[h:h + 1, 0:1], cn_scr[h])


def _mlstm(proj, s0, conv_w, gate_b):
    nrb = N_TOK // HALO
    per = TILE // HALO
    sched = tuple(jnp.asarray(a) for a in _SCHED)
    cb = COL_B // 1024
    grid_spec = pltpu.PrefetchScalarGridSpec(
        num_scalar_prefetch=6,
        grid=(2, N_TILES),
        in_specs=[
            pl.BlockSpec((TILE, 1024), lambda d, i, t, *_: (t[d * N_TILES + i], cb)),
            pl.BlockSpec((HALO, 1024), lambda d, i, t, *_: (jnp.maximum(t[d * N_TILES + i] * per - 1, 0), cb)),
            pl.BlockSpec((HALO, 1024), lambda d, i, t, *_: (jnp.minimum(t[d * N_TILES + i] * per + per, nrb - 1), cb)),
            pl.BlockSpec((TILE, 128), lambda d, i, t, *_: (t[d * N_TILES + i], COL_BG // 128 + d)),
            pl.BlockSpec((1, 1, B_HEADS, B_DK, 128), lambda d, i, t, s, *_: (s[d * N_TILES + i], d, 0, 0, 0)),
            pl.BlockSpec((3, 512), lambda d, i, *_: (0, 0)),
            pl.BlockSpec((1, 1, 128), lambda d, i, *_: (d, 0, 0)),
        ],
        out_specs=[
            pl.BlockSpec((1, TILE, B_W), lambda d, i, t, *_: (d, t[d * N_TILES + i], 0)),
            pl.BlockSpec((1, 1, B_HEADS, B_DK, 128), lambda d, i, t, s, *_: (s[d * N_TILES + i], d, 0, 0, 0)),
        ],
        scratch_shapes=[pltpu.VMEM((B_HEADS, B_DK, 128), F32), pltpu.VMEM((8, 128), F32)]
        + [pltpu.VMEM((TILE, B_W), F32)] * 3 + [pltpu.VMEM((TILE, 128), F32)] * 2,
    )
    return pl.pallas_call(
        _mlstm_kernel,
        grid_spec=grid_spec,
        out_shape=[jax.ShapeDtypeStruct((2, N_TOK, B_W), F32),
                   jax.ShapeDtypeStruct((N_SEQ, 2, B_HEADS, B_DK, 128), F32)],
        compiler_params=_cparams(("arbitrary", "arbitrary")),
        name="mlstm_mixer",
    )(*sched, proj, proj, proj, proj, s0, conv_w, gate_b)


HEAD_G = 128
ATT_SCALE = (C_NOPE + C_ROPE) ** -0.5


def _rope(x, cos, sin_lo, sin_hi):
    return x * cos + pltpu.roll(x, 16, axis=1) * sin_hi + pltpu.roll(x, HEAD_G - 16, axis=1) * sin_lo


def _mla_pre_kernel(xc_ref, qn_ref, kvn_ref, wuq_ref, cos_ref, slo_ref, shi_ref, q_ref, ckv_ref, kpe_ref):
    q_dn = xc_ref[:, 0:C_Q_LORA]
    qn = q_dn * lax.rsqrt(jnp.mean(q_dn * q_dn, axis=-1, keepdims=True) + RMS_EPS) * qn_ref[...]
    q = _dot(qn.astype(BF16), wuq_ref[...])
    cos, slo, shi = cos_ref[...], slo_ref[...], shi_ref[...]
    for h in range(C_HEADS):
        sl = slice(h * HEAD_G, (h + 1) * HEAD_G)
        q_ref[:, sl] = (_rope(q[:, sl], cos, slo, shi) * ATT_SCALE).astype(BF16)
    kv_dn = xc_ref[:, C_Q_LORA:C_Q_LORA + C_KV_LORA]
    ckv_ref[...] = kv_dn * lax.rsqrt(jnp.mean(kv_dn * kv_dn, axis=-1, keepdims=True) + RMS_EPS) * kvn_ref[...]
    kpe_ref[...] = _rope(xc_ref[:, 384:512], cos, slo, shi)


def _mla_pre(proj, q_norm, kv_norm, wuq, rope_tabs):
    cc = COL_C // 512

    def tab_idx(i):
        return (jnp.where(i < N_PROMPT_TILES, TILES_PER_SAMPLE, (i - N_PROMPT_TILES) % TILES_PER_SAMPLE), 0)

    tab_spec = pl.BlockSpec((TILE, HEAD_G), tab_idx)
    return pl.pallas_call(
        _mla_pre_kernel,
        grid=(N_TILES,),
        in_specs=[pl.BlockSpec((TILE, 512), lambda i: (i, cc)),
                  pl.BlockSpec((1, C_Q_LORA), lambda i: (0, 0)),
                  pl.BlockSpec((1, C_KV_LORA), lambda i: (0, 0)),
                  pl.BlockSpec((C_Q_LORA, C_HEADS * HEAD_G), lambda i: (0, 0)),
                  tab_spec, tab_spec, tab_spec],
        out_specs=[pl.BlockSpec((TILE, C_HEADS * HEAD_G), lambda i: (i, 0)),
                   pl.BlockSpec((TILE, C_KV_LORA), lambda i: (i, 0)),
                   pl.BlockSpec((TILE, HEAD_G), lambda i: (i, 0))],
        out_shape=[jax.ShapeDtypeStruct((N_TOK, C_HEADS * HEAD_G), BF16),
                   jax.ShapeDtypeStruct((N_TOK, C_KV_LORA), F32),
                   jax.ShapeDtypeStruct((N_TOK, HEAD_G), F32)],
        compiler_params=_cparams(("parallel",)),
        name="mla_pre",
    )(proj, q_norm, kv_norm, wuq, *rope_tabs)


def _mla_kv_kernel(ckv_ref, kpe_ref, wuk_ref, wuv_ref, k_ref, v_ref):
    ckv = ckv_ref[...].astype(BF16)
    kn = _dot(ckv, wuk_ref[...])
    kpe = kpe_ref[...]
    for h in range(C_HEADS):
        sl = slice(h * HEAD_G, (h + 1) * HEAD_G)
        k_ref[:, sl] = (kn[:, sl] + kpe).astype(BF16)
    v_ref[...] = _dot(ckv, wuv_ref[...]).astype(BF16)


def _mla_kv(ckv_all, kpe_all, wuk, wuv):
    n = ckv_all.shape[0]
    return pl.pallas_call(
        _mla_kv_kernel,
        grid=(n // TILE,),
        in_specs=[pl.BlockSpec((TILE, C_KV_LORA), lambda i: (i, 0)),
                  pl.BlockSpec((TILE, HEAD_G), lambda i: (i, 0)),
                  pl.BlockSpec((C_KV_LORA, C_HEADS * HEAD_G), lambda i: (0, 0)),
                  pl.BlockSpec((C_KV_LORA, C_W), lambda i: (0, 0))],
        out_specs=[pl.BlockSpec((TILE, C_HEADS * HEAD_G), lambda i: (i, 0)),
                   pl.BlockSpec((TILE, C_W), lambda i: (i, 0))],
        out_shape=[jax.ShapeDtypeStruct((n, C_HEADS * HEAD_G), BF16),
                   jax.ShapeDtypeStruct((n, C_W), BF16)],
        compiler_params=_cparams(("parallel",)),
        name="mla_kv",
    )(ckv_all, kpe_all, wuk, wuv)


def _attn_kernel(q_ref, k_ref, v_ref, o_ref):
    for h in range(C_HEADS):
        sl = slice(h * HEAD_G, (h + 1) * HEAD_G)
        s = _dot(q_ref[:, sl], k_ref[0, :, sl], NT)
        e = jnp.exp(s - jnp.max(s, axis=1, keepdims=True))
        den = jnp.sum(e, axis=1, keepdims=True)
        o = _dot(e.astype(BF16), v_ref[0, :, h * C_V:(h + 1) * C_V])
        o_ref[:, h * C_V:(h + 1) * C_V] = o / den


def _attention(q, k, v, tq):
    n_seq, lk, _ = k.shape
    lq = q.shape[0] // n_seq
    nqb = lq // tq
    return pl.pallas_call(
        _attn_kernel,
        grid=(n_seq, nqb),
        in_specs=[pl.BlockSpec((tq, C_HEADS * HEAD_G), lambda s, j: (s * nqb + j, 0)),
                  pl.BlockSpec((1, lk, C_HEADS * HEAD_G), lambda s, j: (s, 0, 0)),
                  pl.BlockSpec((1, lk, C_W), lambda s, j: (s, 0, 0))],
        out_specs=pl.BlockSpec((tq, C_W), lambda s, j: (s * nqb + j, 0)),
        out_shape=jax.ShapeDtypeStruct((q.shape[0], C_W), F32),
        compiler_params=_cparams(("parallel", "parallel")),
        name="mla_attention",
    )(q, k, v)


def _merge_kernel(x_ref, rw0_ref, rw1_ref, ml0_ref, ml1_ref, yc_ref, ga_ref, gb_ref, gc_ref, og_ref,
                  g1_ref, sh2_ref, sc2_ref, rwlw_ref, rwlb_ref, mllw_ref, mllb_ref,
                  pa_ref, pb_ref, pc_ref, wo_ref, l1w_ref, l1b_ref, rtw_ref, rtb_ref,
                  x1_ref, u2_ref, lg_ref):
    ones_blk = _block_ones(A_W, A_HEAD)
    rw0 = rw0_ref[...]
    rw1 = rw1_ref[...]
    ya = _head_norm(rw0[:, 0:256] + rw1[:, 0:256], ones_blk, rwlw_ref[...], rwlb_ref[...], A_GN_EPS)
    ya = (ya + rw0[:, 256:512] + rw1[:, 256:512]) * rw0[:, 512:768]
    yb = _head_norm(ml0_ref[0] + ml1_ref[0], ones_blk, mllw_ref[...], mllb_ref[...], LN_EPS)
    yb = yb * _sigmoid(og_ref[...])
    merged = (_sigmoid(ga_ref[...]) * _dot(ya.astype(BF16), pa_ref[...])
              + _sigmoid(gb_ref[...]) * _dot(yb.astype(BF16), pb_ref[...])
              + _sigmoid(gc_ref[...]) * _dot(yc_ref[...].astype(BF16), pc_ref[...]))
    mix = _dot(merged.astype(BF16), wo_ref[...])
    x1 = _layer_norm(DN_ALPHA * x_ref[...] + g1_ref[0] * mix, l1w_ref[...], l1b_ref[...])
    x1_ref[...] = x1
    u2 = x1 * (1.0 + sc2_ref[0]) + sh2_ref[0]
    u2_ref[...] = u2
    lg_ref[...] = _mm(u2, rtw_ref[...], 3) + rtb_ref[...]


def _merge(x, rw, ml, yc, proj, modt, rwlw, rwlb, mllw, mllb, pa, pb, pc, wo, l1w, l1b, rtw, rtb):
    gcol = COL_G // 1024

    def row(shape):
        return pl.BlockSpec(shape, lambda i: (0, 0))

    def mod(kk):
        return pl.BlockSpec((1, 1, D_MODEL), lambda i: (i, 0, kk))

    return pl.pallas_call(
        _merge_kernel,
        grid=(N_TILES,),
        in_specs=[pl.BlockSpec((TILE, D_MODEL), lambda i: (i, 0)),
                  pl.BlockSpec((TILE, 768), lambda i: (i, 0)),
                  pl.BlockSpec((TILE, 768), lambda i: (i, 0)),
                  pl.BlockSpec((1, TILE, B_W), lambda i: (0, i, 0)),
                  pl.BlockSpec((1, TILE, B_W), lambda i: (1, i, 0)),
                  pl.BlockSpec((TILE, C_W), lambda i: (i, 0)),
                  pl.BlockSpec((TILE, D_MODEL), lambda i: (i, gcol)),
                  pl.BlockSpec((TILE, D_MODEL), lambda i: (i, gcol + 1)),
                  pl.BlockSpec((TILE, D_MODEL), lambda i: (i, gcol + 2)),
                  pl.BlockSpec((TILE, B_W), lambda i: (i, (COL_B + 768) // B_W)),
                  mod(2), mod(3), mod(4),
                  row((1, A_W)), row((1, A_W)), row((1, B_W)), row((1, B_W)),
                  row((A_W, D_MODEL)), row((B_W, D_MODEL)), row((C_W, D_MODEL)), row((D_MODEL, D_MODEL)),
                  row((1, D_MODEL)), row((1, D_MODEL)), row((D_MODEL, 128)), row((1, 128))],
        out_specs=[pl.BlockSpec((TILE, D_MODEL), lambda i: (i, 0)),
                   pl.BlockSpec((TILE, D_MODEL), lambda i: (i, 0)),
                   pl.BlockSpec((TILE, 128), lambda i: (i, 0))],
        out_shape=[jax.ShapeDtypeStruct((N_TOK, D_MODEL), F32),
                   jax.ShapeDtypeStruct((N_TOK, D_MODEL), F32),
                   jax.ShapeDtypeStruct((N_TOK, 128), F32)],
        compiler_params=_cparams(("parallel",)),
        name="merge_postnorm_router",
    )(x, rw[0], rw[1], ml, ml, yc, proj, proj, proj, proj, modt, modt, modt,
      rwlw, rwlb, mllw, mllb, pa, pb, pc, wo, l1w, l1b, rtw, rtb)


def _moe_kernel(be_s, x_ref, w1g_ref, w1l_ref, b1g_ref, b1l_ref, w2_ref, b2_ref, rw_ref, y_ref):
    x = x_ref[...].astype(BF16)
    hg = jnp.minimum(_dot(x, w1g_ref[0, 0]) + b1g_ref[0, 0], SWIGLU_LIMIT)
    hl = jnp.clip(_dot(x, w1l_ref[0, 0]) + b1l_ref[0, 0], -SWIGLU_LIMIT, SWIGLU_LIMIT)
    act = hg * _sigmoid(SWIGLU_ALPHA * hg) * (hl + 1.0)
    y = _dot(act.astype(BF16), w2_ref[0, 0].astype(BF16)) + b2_ref[0, 0]
    y_ref[...] = y * rw_ref[...]


def _moe_experts(l, x_rows, block_e, w1g, w1l, b1g, b1l, w2, b2, row_w):
    n_blocks = x_rows.shape[0] // MOE_BLOCK

    def wspec(shape):
        return pl.BlockSpec((1, 1) + shape, lambda i, be: (l, be[i], 0, 0))

    grid_spec = pltpu.PrefetchScalarGridSpec(
        num_scalar_prefetch=1,
        grid=(n_blocks,),
        in_specs=[pl.BlockSpec((MOE_BLOCK, D_MODEL), lambda i, be: (i, 0)),
                  wspec((D_MODEL, D_EXPERT)), wspec((D_MODEL, D_EXPERT)),
                  wspec((1, D_EXPERT)), wspec((1, D_EXPERT)),
                  wspec((D_EXPERT, D_MODEL)), wspec((1, D_MODEL)),
                  pl.BlockSpec((MOE_BLOCK, 1), lambda i, be: (i, 0))],
        out_specs=pl.BlockSpec((MOE_BLOCK, D_MODEL), lambda i, be: (i, 0)),
    )
    return pl.pallas_call(
        _moe_kernel,
        grid_spec=grid_spec,
        out_shape=jax.ShapeDtypeStruct((x_rows.shape[0], D_MODEL), F32),
        compiler_params=_cparams(("arbitrary",)),
        name="moe_experts",
    )(block_e, x_rows, w1g, w1l, b1g, b1l, w2, b2, row_w)


def _final_kernel(x_ref, f_ref, g2_ref, w_ref, b_ref, o_ref):
    o_ref[...] = _layer_norm(DN_ALPHA * x_ref[...] + g2_ref[0] * f_ref[...], w_ref[...], b_ref[...])


def _final_norm(x1, ffn, modt, w, b):
    return pl.pallas_call(
        _final_kernel,
        grid=(N_TILES,),
        in_specs=[pl.BlockSpec((TILE, D_MODEL), lambda i: (i, 0)),
                  pl.BlockSpec((TILE, D_MODEL), lambda i: (i, 0)),
                  pl.BlockSpec((1, 1, D_MODEL), lambda i: (i, 0, 5)),
                  pl.BlockSpec((1, D_MODEL), lambda i: (0, 0)),
                  pl.BlockSpec((1, D_MODEL), lambda i: (0, 0))],
        out_specs=pl.BlockSpec((TILE, D_MODEL), lambda i: (i, 0)),
        out_shape=jax.ShapeDtypeStruct((N_TOK, D_MODEL), F32),
        compiler_params=_cparams(("parallel",)),
        name="ffn_postnorm",
    )(x1, ffn, modt, w, b)


def _route(logits):
    n_rows = logits.shape[0]
    top_v, top_e = lax.top_k(logits, TOP_K)
    top_w = jax.nn.softmax(top_v, axis=-1)
    n_assign = n_rows * TOP_K
    flat_e = top_e.reshape(-1)
    flat_tok = jnp.repeat(jnp.arange(n_rows, dtype=jnp.int32), TOP_K)
    order = jnp.argsort(flat_e)
    e_sorted = flat_e[order]
    counts = jnp.bincount(flat_e, length=N_EXPERTS)
    padded = (counts + MOE_BLOCK - 1) // MOE_BLOCK * MOE_BLOCK
    pad_end = jnp.cumsum(padded)
    pad_start = pad_end - padded
    start = jnp.cumsum(counts) - counts
    dest = (pad_start[e_sorted] + jnp.arange(n_assign, dtype=jnp.int32) - start[e_sorted]).astype(jnp.int32)
    n_blocks = -(-n_assign // MOE_BLOCK) + N_EXPERTS
    n_pad = n_blocks * MOE_BLOCK
    row_tok = jnp.full((n_pad,), n_rows, jnp.int32).at[dest].set(flat_tok[order])
    row_w = jnp.zeros((n_pad,), F32).at[dest].set(top_w.reshape(-1)[order])
    block_start = jnp.arange(n_blocks, dtype=jnp.int32) * MOE_BLOCK
    block_e = jnp.minimum(jnp.sum(pad_end[None, :] <= block_start[:, None], axis=-1), N_EXPERTS - 1).astype(jnp.int32)
    pos = jnp.zeros((n_assign,), jnp.int32).at[order].set(dest).reshape(n_rows, TOP_K)
    return row_tok, row_w, block_e, pos


def _rope_tables():
    rows = DEC_SEQ // GRID_W
    r, col = jnp.meshgrid(jnp.arange(rows, dtype=F32), jnp.arange(GRID_W, dtype=F32), indexing='ij')
    n_freq = C_ROPE // 4
    inv = 1.0 / (ROPE_BASE ** (jnp.arange(n_freq, dtype=F32) / n_freq))
    ang = jnp.concatenate([r.reshape(-1, 1) * inv, col.reshape(-1, 1) * inv], axis=-1)
    cos, sin = jnp.cos(ang), jnp.sin(ang)
    half = C_ROPE // 2
    one = jnp.ones((DEC_SEQ, C_NOPE), F32)
    zero = jnp.zeros((DEC_SEQ, C_NOPE), F32)
    tail1 = jnp.ones((DEC_SEQ, HEAD_G - C_NOPE - C_ROPE), F32)
    tail0 = jnp.zeros((DEC_SEQ, HEAD_G - C_NOPE - C_ROPE), F32)
    zh = jnp.zeros((DEC_SEQ, half), F32)
    t_cos = jnp.concatenate([one, cos, cos, tail1], axis=1)
    t_lo = jnp.concatenate([zero, -sin, zh, tail0], axis=1)
    t_hi = jnp.concatenate([zero, zh, sin, tail0], axis=1)
    ident = jnp.ones((TILE, HEAD_G), F32)
    nil = jnp.zeros((TILE, HEAD_G), F32)
    return (jnp.concatenate([t_cos, ident]), jnp.concatenate([t_lo, nil]), jnp.concatenate([t_hi, nil]))


def _pad_heads(w, n_heads, width):
    lead = w.shape[:-1]
    w = w.reshape(lead + (n_heads, width))
    w = jnp.pad(w, [(0, 0)] * len(lead) + [(0, 0), (0, HEAD_G - width)])
    return w.reshape(lead + (n_heads * HEAD_G,))


def kernel(x_prompt, x_sample, state_rwkv, state_mlstm_c, state_mlstm_n, state_mlstm_m, cache_mla_ckv,
           cache_mla_kpe, c, c_ctx, ada_w, ada_b, w_in, rw_mu, rw_w0, rw_w2, rw_a0, rw_a2, rw_g2, rw_kk,
           rw_ka, rw_rk, rw_ln_w, rw_ln_b, ml_conv, ml_gate_b, ml_ln_w, ml_ln_b, mla_q_norm, mla_wuq,
           mla_kv_norm, mla_wuk, mla_wuv, proj_a, proj_b, proj_c, w_out, ln1_w, ln1_b, router_w, router_b,
           moe_w1, moe_b1, moe_w2, moe_b2, ln2_w, ln2_b):
    L = DEPTH
    x = jnp.concatenate([x_prompt.reshape(-1, D_MODEL), x_sample.reshape(-1, D_MODEL)], axis=0)

    cond = jnp.concatenate([c_ctx[None], c, jnp.zeros((8 - 1 - DEC_BATCH, D_MODEL), F32)], axis=0)
    wa = w_in[:, :, 0:A_IN]
    wb = w_in[:, :, A_IN:A_IN + 1024]
    wbg = w_in[:, :, A_IN + 1024:A_IN + B_IN]
    wc = w_in[:, :, A_IN + B_IN:A_IN + B_IN + C_IN]
    wg = w_in[:, :, A_IN + B_IN + C_IN:]
    z = lambda n: jnp.zeros((L, D_MODEL, n), F32)
    w_proj = jnp.concatenate([
        wa, wg, wb,
        wc[:, :, 0:384], z(64), wc[:, :, 384:416], z(32),
        wbg[:, :, 0:8], z(120), wbg[:, :, 8:16], z(120)], axis=2).astype(BF16)
    gate_b = jnp.pad(ml_gate_b.reshape(L, 2, 1, 2 * B_HEADS), ((0, 0), (0, 0), (0, 0), (0, 128 - 2 * B_HEADS)))
    wuq = _pad_heads(mla_wuq, C_HEADS, C_NOPE + C_ROPE).astype(BF16)
    wuk = _pad_heads(mla_wuk, C_HEADS, C_NOPE).astype(BF16)
    wuv = mla_wuv.astype(BF16)
    pa, pb, pc, wo = proj_a.astype(BF16), proj_b.astype(BF16), proj_c.astype(BF16), w_out.astype(BF16)
    rtw = jnp.pad(router_w, ((0, 0), (0, 0), (0, 128 - N_EXPERTS)))
    rtb = jnp.pad(router_b, ((0, 0), (0, 128 - N_EXPERTS))).reshape(L, 1, 128)
    w1 = moe_w1.reshape(L, N_EXPERTS, D_MODEL, D_EXPERT, 2)
    w1g, w1l = w1[..., 0].astype(BF16), w1[..., 1].astype(BF16)
    b1 = moe_b1.reshape(L, N_EXPERTS, 1, D_EXPERT, 2)
    b1g, b1l = b1[..., 0], b1[..., 1]
    b2 = moe_b2.reshape(L, N_EXPERTS, 1, D_MODEL)
    rope_tabs = _rope_tables()
    row2 = lambda a: a.reshape(L, 1, -1)

    rw_s0 = jnp.concatenate([jnp.zeros((BATCH, L, 2, A_HEADS, A_HEAD, A_HEAD), F32),
                             jnp.swapaxes(state_rwkv, -1, -2)], axis=0)
    m_col = jnp.broadcast_to(state_mlstm_m[..., None, None], state_mlstm_m.shape + (B_DK, 1))
    ml_dec = jnp.concatenate([state_mlstm_c, state_mlstm_n[..., None], m_col,
                              jnp.zeros(state_mlstm_m.shape + (B_DK, 128 - B_DK - 2), F32)], axis=-1)
    ml_s0 = jnp.concatenate([jnp.zeros((BATCH,) + ml_dec.shape[1:], F32), ml_dec], axis=0)
    kpe_cache = jnp.pad(cache_mla_kpe, ((0, 0), (0, 0), (0, 0), (C_NOPE, HEAD_G - C_NOPE - C_ROPE)))

    mod = _modulation(cond, ada_w, ada_b)
    n_prompt = BATCH * SEQ
    outs = {k: [] for k in ('rw', 'mlc', 'ckv', 'kpe')}
    for l in range(L):
        modt = mod[l][_MOD_ROW_OF_TILE].reshape(N_TILES, 1, 6 * D_MODEL)
        proj = _in_projection(x, modt, w_proj[l])
        rw, rw_fin = _rwkv(proj, rw_s0[:, l], row2(rw_mu)[l], rw_w0[l][:, None], rw_w2[l], rw_a0[l][:, None],
                           rw_a2[l], rw_g2[l], row2(rw_kk)[l], row2(rw_ka)[l], row2(rw_rk)[l])
        ml, ml_fin = _mlstm(proj, ml_s0[:, l], ml_conv[l], gate_b[l])
        q, ckv, kpe = _mla_pre(proj, row2(mla_q_norm)[l], row2(mla_kv_norm)[l], wuq[l], rope_tabs)
        ckv_all = jnp.concatenate([ckv[:n_prompt]] + [
            t for s in range(DEC_BATCH)
            for t in (cache_mla_ckv[s, l], ckv[n_prompt + s * DEC_SEQ:n_prompt + (s + 1) * DEC_SEQ])], axis=0)
        kpe_all = jnp.concatenate([kpe[:n_prompt]] + [
            t for s in range(DEC_BATCH)
            for t in (kpe_cache[s, l], kpe[n_prompt + s * DEC_SEQ:n_prompt + (s + 1) * DEC_SEQ])], axis=0)
        kf, vf = _mla_kv(ckv_all, kpe_all, wuk[l], wuv[l])
        lk = PAST_LEN + DEC_SEQ
        yc_p = _attention(q[:n_prompt], kf[:n_prompt].reshape(BATCH, SEQ, -1),
                          vf[:n_prompt].reshape(BATCH, SEQ, -1), SEQ)
        yc_s = _attention(q[n_prompt:], kf[n_prompt:].reshape(DEC_BATCH, lk, -1),
                          vf[n_prompt:].reshape(DEC_BATCH, lk, -1), 256)
        yc = jnp.concatenate([yc_p, yc_s], axis=0)
        x1, u2, logits = _merge(x, rw, ml, yc, proj, modt, row2(rw_ln_w)[l], row2(rw_ln_b)[l],
                                row2(ml_ln_w)[l], row2(ml_ln_b)[l], pa[l], pb[l], pc[l], wo[l],
                                row2(ln1_w)[l], row2(ln1_b)[l], rtw[l], rtb[l])
        row_tok, row_w, block_e, pos = _route(logits[:, :N_EXPERTS])
        x_rows = jnp.concatenate([u2, jnp.zeros((1, D_MODEL), F32)], axis=0)[row_tok]
        y_rows = _moe_experts(l, x_rows, block_e, w1g, w1l, b1g, b1l, moe_w2, b2, row_w[:, None])
        ffn = y_rows[pos].sum(axis=1)
        x = _final_norm(x1, ffn, modt, row2(ln2_w)[l], row2(ln2_b)[l])
        outs['rw'].append(jnp.swapaxes(jnp.stack([rw_fin[0][:BATCH], rw_fin[1][:BATCH]], axis=1), -1, -2))
        outs['mlc'].append(ml_fin[:BATCH])
        outs['ckv'].append(ckv[:n_prompt].reshape(BATCH, SEQ, C_KV_LORA))
        outs['kpe'].append(kpe[:n_prompt, C_NOPE:C_NOPE + C_ROPE].reshape(BATCH, SEQ, C_ROPE))

    stack = lambda k: jnp.stack(outs[k], axis=1)
    mlc = stack('mlc')
    return (x[:n_prompt].reshape(BATCH, SEQ, D_MODEL), x[n_prompt:].reshape(DEC_BATCH, DEC_SEQ, D_MODEL),
            stack('rw'), mlc[..., 0:B_DK], mlc[..., N_COL], mlc[..., 0, M_COL], stack('ckv'), stack('kpe'))
```

```python
import functools

import numpy as np
import jax
import jax.numpy as jnp
from jax import lax
from jax.experimental import pallas as pl
from jax.experimental.pallas import tpu as pltpu

F32 = jnp.float32
BF16 = jnp.bfloat16

D_MODEL = 1024
BATCH = 16
SEQ = 256
DEPTH = 4
DEC_BATCH = 2
DEC_SEQ = 4096
PAST_LEN = 256
GRID_W = 64

A_HEADS = 4
A_HEAD = 64
A_W = 256
A_GN_EPS = 64e-5
B_HEADS = 4
B_DK = 64
B_W = 256
C_HEADS = 8
C_NOPE = 64
C_ROPE = 32
C_V = 64
C_Q_LORA = 256
C_KV_LORA = 128
C_W = 512
ROPE_BASE = 10000.0
N_EXPERTS = 32
TOP_K = 4
D_EXPERT = 1024
SWIGLU_LIMIT = 7.0
SWIGLU_ALPHA = 1.702
MOE_BLOCK = 256
A_IN = 1024
B_IN = 1040
C_IN = 416
DN_ALPHA = (2 * DEPTH) ** 0.25
LN_EPS = 1e-5
RMS_EPS = 1e-6

TILE = 256
CHUNK = 64
N_TOK = BATCH * SEQ + DEC_BATCH * DEC_SEQ
N_TILES = N_TOK // TILE
N_PROMPT_TILES = BATCH * SEQ // TILE
TILES_PER_SAMPLE = DEC_SEQ // TILE
N_SEQ = BATCH + DEC_BATCH
HALO = 8

COL_A = 0
COL_G = 1024
COL_B = 4096
COL_C = 5120
COL_BG = 5632
N_PROJ = 5888

VMEM_LIMIT = 56 * 1024 * 1024


def _cparams(sem):
    return pltpu.CompilerParams(dimension_semantics=sem, vmem_limit_bytes=VMEM_LIMIT)


NN = ((1,), (0,))
NT = ((1,), (1,))
TN = ((0,), (0,))


def _dot(a, b, dims=NN):
    return lax.dot_general(a, b, (dims, ((), ())), preferred_element_type=F32)


def _split(x, n):
    parts, r = [], x
    for i in range(n):
        p = r.astype(BF16)
        parts.append(p)
        if i + 1 < n:
            r = r - p.astype(F32)
    return parts


def _mm(a, b, passes=1, dims=NN):
    if passes == 1:
        return _dot(a.astype(BF16), b.astype(BF16), dims)
    ah, al = _split(a, 2)
    bh, bl = _split(b, 2)
    return _dot(ah, bh, dims) + (_dot(al, bh, dims) + _dot(ah, bl, dims))


def _mm_exact_l(t01, x):
    x0, x1, x2 = _split(x, 3)
    return _dot(t01, x0) + (_dot(t01, x1) + _dot(t01, x2))


def _mm_exact_r(x, t01):
    x0, x1, x2 = _split(x, 3)
    return _dot(x0, t01) + (_dot(x1, t01) + _dot(x2, t01))


def _sigmoid(x):
    return 1.0 / (1.0 + jnp.exp(-x))


def _softplus(x):
    return jnp.maximum(x, 0.0) + jnp.log(1.0 + jnp.exp(-jnp.abs(x)))


def _block_ones(n, blk):
    r = lax.broadcasted_iota(jnp.int32, (n, n), 0) // blk
    c = lax.broadcasted_iota(jnp.int32, (n, n), 1) // blk
    return jnp.where(r == c, 1.0, 0.0).astype(BF16)


def _layer_norm(x, w, b):
    mu = jnp.mean(x, axis=-1, keepdims=True)
    xc = x - mu
    var = jnp.mean(xc * xc, axis=-1, keepdims=True)
    return xc * lax.rsqrt(var + LN_EPS) * w + b


def _head_norm(x, ones_blk, w, b, eps):
    mu = _mm_exact_r(x, ones_blk) * (1.0 / 64.0)
    xc = x - mu
    var = _mm_exact_r(xc * xc, ones_blk) * (1.0 / 64.0)
    return xc * lax.rsqrt(var + eps) * w + b


def _shifted(x, prev_row, next_row):
    rid = lax.broadcasted_iota(jnp.int32, x.shape, 0)
    xp = jnp.where(rid == 0, prev_row, pltpu.roll(x, 1, axis=0))
    xn = jnp.where(rid == x.shape[0] - 1, next_row, pltpu.roll(x, x.shape[0] - 1, axis=0))
    return xp, xn


def _schedule():
    seq_of_tile = np.concatenate([np.arange(BATCH), BATCH + np.repeat(np.arange(DEC_BATCH), TILES_PER_SAMPLE)])
    first_of_tile = np.ones(N_TILES, np.int32)
    last_of_tile = np.ones(N_TILES, np.int32)
    for s in range(DEC_BATCH):
        base = N_PROMPT_TILES + s * TILES_PER_SAMPLE
        first_of_tile[base + 1: base + TILES_PER_SAMPLE] = 0
        last_of_tile[base: base + TILES_PER_SAMPLE - 1] = 0
    tile_fwd = np.arange(N_TILES)
    tile_bwd = np.arange(N_TILES)
    for s in range(DEC_BATCH):
        base = N_PROMPT_TILES + s * TILES_PER_SAMPLE
        tile_bwd[base: base + TILES_PER_SAMPLE] = base + TILES_PER_SAMPLE - 1 - np.arange(TILES_PER_SAMPLE)
    tile = np.concatenate([tile_fwd, tile_bwd]).astype(np.int32)
    seq = seq_of_tile[tile].astype(np.int32)
    begins = np.concatenate([first_of_tile[tile_fwd], last_of_tile[tile_bwd]]).astype(np.int32)
    ends = np.concatenate([last_of_tile[tile_fwd], first_of_tile[tile_bwd]]).astype(np.int32)
    has_prev = (1 - first_of_tile).astype(np.int32)
    has_next = (1 - last_of_tile).astype(np.int32)
    return tile, seq, begins, ends, has_prev, has_next


_SCHED = _schedule()
_SEQ_OF_TILE = np.concatenate([np.arange(BATCH), BATCH + np.repeat(np.arange(DEC_BATCH), TILES_PER_SAMPLE)]).astype(np.int32)
_MOD_ROW_OF_TILE = np.concatenate([np.zeros(N_PROMPT_TILES), 1 + np.repeat(np.arange(DEC_BATCH), TILES_PER_SAMPLE)]).astype(np.int32)


def _direction_masks(d):
    row = lax.broadcasted_iota(jnp.int32, (CHUNK, CHUNK), 0)
    col = lax.broadcasted_iota(jnp.int32, (CHUNK, CHUNK), 1)
    diff = (row - col) * (1 - 2 * d)
    return diff >= 0, diff > 0, diff <= 0


def _mod_kernel(c_ref, w_ref, b_ref, o_ref):
    c = c_ref[...]
    o_ref[0] = _mm(c * _sigmoid(c), w_ref[0], 3) + b_ref[0]


def _modulation(cond, ada_w, ada_b):
    tn = 1536
    return pl.pallas_call(
        _mod_kernel,
        grid=(DEPTH, 6 * D_MODEL // tn),
        in_specs=[pl.BlockSpec((8, D_MODEL), lambda l, j: (0, 0)),
                  pl.BlockSpec((1, D_MODEL, tn), lambda l, j: (l, 0, j)),
                  pl.BlockSpec((1, 1, tn), lambda l, j: (l, 0, j))],
        out_specs=pl.BlockSpec((1, 8, tn), lambda l, j: (l, 0, j)),
        out_shape=jax.ShapeDtypeStruct((DEPTH, 8, 6 * D_MODEL), F32),
        compiler_params=_cparams(("parallel", "parallel")),
        name="ada_modulation",
    )(cond, ada_w, ada_b.reshape(DEPTH, 1, 6 * D_MODEL))


IN_TM = 512
IN_TN = N_PROJ // 2


def _inproj_kernel(x_ref, sh_ref, sc_ref, w_ref, o_ref):
    u = x_ref[...] * (1.0 + sc_ref[0]) + sh_ref[0]
    o_ref[...] = _dot(u.astype(BF16), w_ref[...])


def _in_projection(x, modt, w):
    rep = IN_TM // TILE
    return pl.pallas_call(
        _inproj_kernel,
        grid=(N_PROJ // IN_TN, N_TOK // IN_TM),
        in_specs=[pl.BlockSpec((IN_TM, D_MODEL), lambda j, i: (i, 0)),
                  pl.BlockSpec((1, 1, D_MODEL), lambda j, i: (rep * i, 0, 0)),
                  pl.BlockSpec((1, 1, D_MODEL), lambda j, i: (rep * i, 0, 1)),
                  pl.BlockSpec((D_MODEL, IN_TN), lambda j, i: (0, j))],
        out_specs=pl.BlockSpec((IN_TM, IN_TN), lambda j, i: (i, j)),
        out_shape=jax.ShapeDtypeStruct((N_TOK, N_PROJ), F32),
        compiler_params=_cparams(("parallel", "parallel")),
        name="in_projection",
    )(x, modt, modt, w)


RW_P = 3
RW_INV_P = 1
HB = A_HEADS * CHUNK


def _pre(x, passes):
    return tuple(_split(x, 2)) if passes == 3 else (x.astype(BF16),)


def _mmp(a, b, dims=NN):
    out = _dot(a[0], b[0], dims)
    if len(a) == 2 and len(b) == 2:
        out = out + (_dot(a[1], b[0], dims) + _dot(a[0], b[1], dims))
    return out


def _head_expand(m, same_head):
    return jnp.where(same_head, jnp.concatenate([m] * A_HEADS, axis=0), 0.0)


def _head_stack(m):
    return jnp.concatenate([m[:, h * A_HEAD:(h + 1) * A_HEAD] for h in range(A_HEADS)], axis=0)


def _head_unstack(m):
    return jnp.concatenate([m[h * CHUNK:(h + 1) * CHUNK, :] for h in range(A_HEADS)], axis=1)


def _rwkv_kernel(d, tile_s, seq_s, begin_s, end_s, hasprev_s, hasnext_s,
                 xa_ref, xp_ref, xn_ref, s0_ref, mu_ref, w0_ref, w2_ref, a0_ref, a2_ref, g2_ref,
                 kk_ref, ka_ref, rk_ref,
                 out_ref, sfin_ref,
                 st_scr):
    step = pl.program_id(0)
    tile = tile_s[step]

    @pl.when(begin_s[step] == 1)
    def _():
        st_scr[...] = s0_ref[0]

    x = xa_ref[...]
    prev_row = xp_ref[HALO - 1:HALO, :] * hasprev_s[tile].astype(F32)
    next_row = xn_ref[0:1, :] * hasnext_s[tile].astype(F32)
    xp, xn = _shifted(x, prev_row, next_row)
    xs = x + mu_ref[...] * (0.5 * (xp + xn) - x)
    r = xs[:, 0:256]
    k = xs[:, 256:512]
    v = xs[:, 512:768]
    wl = xs[:, 768:832]
    al = xs[:, 832:896]
    gl = xs[:, 896:1024]

    ones_blk = _block_ones(A_W, A_HEAD)
    kkv = k * kk_ref[...]
    nrm = jnp.sqrt(_mm_exact_r(kkv * kkv, ones_blk))
    kap = kkv / jnp.maximum(nrm, 1e-6)
    wpre = w0_ref[...] + _mm(jnp.tanh(wl), w2_ref[...], 3)
    lw = -jnp.exp(-_softplus(-wpre) - 0.5)
    a = _sigmoid(a0_ref[...] + _mm(al, a2_ref[...], 3))
    kd = k * (1.0 + (a - 1.0) * ka_ref[...])
    bonus = _mm_exact_r(r * kd * rk_ref[...], ones_blk) * v
    out_ref[:, 256:512] = bonus
    out_ref[:, 512:768] = _mm(_sigmoid(gl), g2_ref[...], 3)

    b = kap * a

    incl64, _, _ = _direction_masks(d)
    tinc = jnp.where(incl64, 1.0, 0.0).astype(BF16)
    row = lax.broadcasted_iota(jnp.int32, (HB, HB), 0)
    col = lax.broadcasted_iota(jnp.int32, (HB, HB), 1)
    same_head = (row // CHUNK) == (col // CHUNK)
    order = jnp.where(same_head, (row % CHUNK - col % CHUNK) * (1 - 2 * d), -1)
    incl = order >= 0
    strict = order > 0
    eye = jnp.where(row == col, 1.0, 0.0)
    xor = row ^ col
    level_masks = [(xor >> s) == 1 for s in range(CHUNK.bit_length() - 1)]
    n_chunks = TILE // CHUNK
    chunks = [c if d == 0 else n_chunks - 1 - c for c in range(n_chunks)]
    sl = [slice(c * CHUNK, (c + 1) * CHUNK) for c in chunks]
    rng = range(n_chunks)

    lwc = [lw[s] for s in sl]
    cs = [_mm_exact_l(tinc, x) for x in lwc]
    w_in = [jnp.exp(x) for x in cs]
    w_ex = [jnp.exp(cs[i] - lwc[i]) for i in rng]
    w_inv = [jnp.exp(-x) for x in cs]
    w_tot = [jnp.exp(jnp.sum(x, axis=0, keepdims=True)) for x in lwc]
    kt = [_head_expand(kap[sl[i]] * w_ex[i], same_head) for i in rng]
    rt = [_head_expand(r[sl[i]] * w_in[i], same_head) for i in rng]
    bt = [b[sl[i]] * w_inv[i] for i in rng]
    kdt = [kd[sl[i]] * w_inv[i] for i in rng]
    v_st = [_pre(_head_stack(v[s]), RW_P) for s in sl]
    gram = [_mm(jnp.concatenate([kt[i], rt[i]], axis=0),
                jnp.concatenate([bt[i]] * A_HEADS + [kdt[i]] * A_HEADS, axis=0), RW_P, NT) for i in rng]
    l_b = [jnp.where(strict, g[:HB, :HB], 0.0) for g in gram]
    l_k = [jnp.where(strict, g[:HB, HB:], 0.0) for g in gram]
    m_b = [_pre(jnp.where(incl, g[HB:, :HB], 0.0), RW_P) for g in gram]
    m_k = [jnp.where(incl, g[HB:, HB:], 0.0) for g in gram]
    t_inv = [eye - jnp.where(level_masks[0], x, 0.0) for x in l_b]
    for mask in level_masks[1:]:
        tp = [_pre(t, RW_INV_P) for t in t_inv]
        w1 = [_mmp(tp[i], _pre(jnp.where(mask, l_b[i], 0.0), RW_INV_P)) for i in rng]
        t_inv = [t_inv[i] - _mmp(_pre(w1[i], RW_INV_P), tp[i]) for i in rng]
    lkv = [_mmp(_pre(l_k[i], RW_P), v_st[i]) for i in rng]
    xx = [_mm(t_inv[i], jnp.concatenate([kt[i], lkv[i]], axis=1), RW_P) for i in rng]
    xxp = [_pre(x, RW_P) for x in xx]
    mx = [_mmp(m_b[i], xxp[i]) for i in rng]
    rhat = [_pre(rt[i] - mx[i][:, :HB], RW_P) for i in rng]
    y0 = [_mmp(_pre(m_k[i], RW_P), v_st[i]) - mx[i][:, HB:] for i in rng]
    bhp = [_pre(_head_expand(bt[i] * w_tot[i], same_head), RW_P) for i in rng]
    bx = [_mmp(bhp[i], xxp[i], TN) for i in rng]
    g_mat = [_pre(eye * w_tot[i] - bx[i][:, :HB], RW_P) for i in rng]
    h_mat = [_mmp(_pre(_head_expand(kdt[i] * w_tot[i], same_head), RW_P), v_st[i], TN) - bx[i][:, HB:] for i in rng]

    st = st_scr[...]
    for i in rng:
        stp = _pre(st, RW_P)
        out_ref[sl[i], 0:256] = _head_unstack(_mmp(rhat[i], stp) + y0[i])
        st = _mmp(g_mat[i], stp) + h_mat[i]
    st_scr[...] = st

    @pl.when(end_s[step] == 1)
    def _():
        sfin_ref[0] = st


def _rwkv_direction(d, proj, s0, mu, w0, w2, a0, a2, g2, kk, ka, rk):
    nrb = N_TOK // HALO
    per = TILE // HALO
    tile, seq, begins, ends, has_prev, has_next = _SCHED
    half = slice(d * N_TILES, (d + 1) * N_TILES)
    sched = tuple(jnp.asarray(a) for a in (tile[half], seq[half], begins[half], ends[half], has_prev, has_next))

    def const(shape):
        return pl.BlockSpec(shape, lambda i, *_: (0,) * len(shape))

    grid_spec = pltpu.PrefetchScalarGridSpec(
        num_scalar_prefetch=6,
        grid=(N_TILES,),
        in_specs=[
            pl.BlockSpec((TILE, A_IN), lambda i, t, *_: (t[i], COL_A // A_IN)),
            pl.BlockSpec((HALO, A_IN), lambda i, t, *_: (jnp.maximum(t[i] * per - 1, 0), 0)),
            pl.BlockSpec((HALO, A_IN), lambda i, t, *_: (jnp.minimum(t[i] * per + per, nrb - 1), 0)),
            pl.BlockSpec((1, HB, A_HEAD), lambda i, t, s, *_: (s[i], 0, 0)),
            const((1, A_IN)),
            const((1, A_W)), const((64, A_W)), const((1, A_W)), const((64, A_W)),
            const((128, A_W)), const((1, A_W)), const((1, A_W)), const((1, A_W)),
        ],
        out_specs=[
            pl.BlockSpec((TILE, 768), lambda i, t, *_: (t[i], 0)),
            pl.BlockSpec((1, HB, A_HEAD), lambda i, t, s, *_: (s[i], 0, 0)),
        ],
        scratch_shapes=[pltpu.VMEM((HB, A_HEAD), F32)],
    )
    return pl.pallas_call(
        functools.partial(_rwkv_kernel, d),
        grid_spec=grid_spec,
        out_shape=[jax.ShapeDtypeStruct((N_TOK, 768), F32),
                   jax.ShapeDtypeStruct((N_SEQ, HB, A_HEAD), F32)],
        compiler_params=_cparams(("arbitrary",)),
        name="rwkv7_mixer_fwd" if d == 0 else "rwkv7_mixer_bwd",
    )(*sched, proj, proj, proj, s0, mu, w0, w2, a0, a2, g2, kk, ka, rk)


def _rwkv(proj, s0, mu, w0, w2, a0, a2, g2, kk, ka, rk):
    outs = [_rwkv_direction(d, proj, s0[:, d].reshape(N_SEQ, HB, A_HEAD), mu, w0[d], w2[d], a0[d], a2[d],
                            g2, kk, ka, rk) for d in range(2)]
    fin = tuple(o[1].reshape(N_SEQ, A_HEADS, A_HEAD, A_HEAD) for o in outs)
    return (outs[0][0], outs[1][0]), fin


ML_P = 3
N_COL = B_DK
M_COL = B_DK + 1
NEG = -1e30


def _mlstm_kernel(tile_s, seq_s, begin_s, end_s, hasprev_s, hasnext_s,
                  xb_ref, xp_ref, xn_ref, gt_ref, s0_ref, cw_ref, gb_ref,
                  out_ref, sfin_ref,
                  cn_scr, m_scr, q_scr, k_scr, v_scr, li_scr, lf_scr):
    d = pl.program_id(0)
    step = d * N_TILES + pl.program_id(1)
    tile = tile_s[step]

    @pl.when(begin_s[step] == 1)
    def _():
        s0 = s0_ref[0, 0]
        cn_scr[...] = s0
        for h in range(B_HEADS):
            m_scr[h:h + 1, :] = jnp.broadcast_to(s0[h, 0:1, M_COL:M_COL + 1], (1, 128))

    qk = xb_ref[:, 0:512]
    prev_row = xp_ref[HALO - 1:HALO, 0:512] * hasprev_s[tile].astype(F32)
    next_row = xn_ref[0:1, 0:512] * hasnext_s[tile].astype(F32)
    qp, qn = _shifted(qk, prev_row, next_row)
    conv = cw_ref[0:1, :] * qp + cw_ref[1:2, :] * qk + cw_ref[2:3, :] * qn
    act = conv * _sigmoid(conv)
    q_scr[...] = act[:, 0:256]
    k_scr[...] = act[:, 256:512] * (B_DK ** -0.5)
    v_scr[...] = xb_ref[:, 512:768]
    g = gt_ref[...] + gb_ref[0]
    li_scr[...] = g
    lf_scr[...] = -_softplus(-g)

    incl, _, incl_t = _direction_masks(d)
    tinc = jnp.where(incl, 1.0, 0.0).astype(BF16)
    tinc_t = jnp.where(incl_t, 1.0, 0.0).astype(BF16)
    lane = lax.broadcasted_iota(jnp.int32, (CHUNK, 128), 1)

    def chunk_body(c, carry):
        ce = c + d * (TILE // CHUNK - 1 - 2 * c)
        off = pl.multiple_of(ce * CHUNK, CHUNK)
        rows = pl.ds(off, CHUNK)
        li_c = li_scr[rows, :]
        lf_c = lf_scr[rows, :]
        li_r = li_c.T
        lf_r = lf_c.T
        b_c = _mm_exact_l(tinc, lf_c)
        b_r = _mm_exact_r(lf_r, tinc_t)
        b_end = jnp.sum(lf_c, axis=0, keepdims=True)
        qc = q_scr[rows, :]
        kc = k_scr[rows, :]
        vc = v_scr[rows, :]
        hs = []
        for h in range(B_HEADS):
            sl = slice(h * B_DK, (h + 1) * B_DK)
            fi = B_HEADS + h
            bc = b_c[:, fi:fi + 1]
            br = b_r[fi:fi + 1, :]
            lir = li_r[h:h + 1, :]
            lic = li_c[:, h:h + 1]
            m_old = m_scr[h:h + 1, 0:1]
            dlog = jnp.where(incl, bc - br + lir, NEG)
            inter = bc + m_old
            mj = jnp.maximum(jnp.max(dlog, axis=1, keepdims=True), inter)
            s = _mm(qc[:, sl], kc[:, sl], ML_P, NT) * jnp.exp(dlog - mj)
            e_int = jnp.exp(inter - mj)
            v_aug = jnp.where(lane == N_COL, 1.0,
                              jnp.concatenate([vc[:, sl], jnp.zeros((CHUNK, 128 - B_DK), F32)], axis=1))
            cn = cn_scr[h]
            nd = _mm(s, v_aug, ML_P) + e_int * _mm(qc[:, sl], cn, ML_P)
            den = jnp.maximum(jnp.abs(nd[:, N_COL:N_COL + 1]), jnp.exp(-mj))
            hs.append(nd[:, 0:B_DK] / den)
            be = b_end[:, fi:fi + 1]
            wlog = be - bc + lic
            m_new = jnp.maximum(be + m_old, jnp.max(wlog, axis=0, keepdims=True))
            wk = jnp.exp(wlog - m_new)
            dec = jnp.exp(be + m_old - m_new)
            cn_scr[h] = dec * cn + _mm(kc[:, sl], wk * v_aug, ML_P, TN)
            m_scr[h:h + 1, :] = jnp.broadcast_to(m_new, (1, 128))
        out_ref[0, rows, :] = jnp.concatenate(hs, axis=1)
        return carry

    lax.fori_loop(0, TILE // CHUNK, chunk_body, 0)

    @pl.when(end_s[step] == 1)
    def _():
        for h in range(B_HEADS):
            sfin_ref[0, 0, h] = jnp.where(lane == M_COL, m_scr[h:h + 1, 0:1], cn_scr[h])


def _mlstm(proj, s0, conv_w, gate_b):
    nrb = N_TOK // HALO
    per = TILE // HALO
    sched = tuple(jnp.asarray(a) for a in _SCHED)
    cb = COL_B // 1024
    grid_spec = pltpu.PrefetchScalarGridSpec(
        num_scalar_prefetch=6,
        grid=(2, N_TILES),
        in_specs=[
            pl.BlockSpec((TILE, 1024), lambda d, i, t, *_: (t[d * N_TILES + i], cb)),
            pl.BlockSpec((HALO, 1024), lambda d, i, t, *_: (jnp.maximum(t[d * N_TILES + i] * per - 1, 0), cb)),
            pl.BlockSpec((HALO, 1024), lambda d, i, t, *_: (jnp.minimum(t[d * N_TILES + i] * per + per, nrb - 1), cb)),
            pl.BlockSpec((TILE, 128), lambda d, i, t, *_: (t[d * N_TILES + i], COL_BG // 128 + d)),
            pl.BlockSpec((1, 1, B_HEADS, B_DK, 128), lambda d, i, t, s, *_: (s[d * N_TILES + i], d, 0, 0, 0)),
            pl.BlockSpec((3, 512), lambda d, i, *_: (0, 0)),
            pl.BlockSpec((1, 1, 128), lambda d, i, *_: (d, 0, 0)),
        ],
        out_specs=[
            pl.BlockSpec((1, TILE, B_W), lambda d, i, t, *_: (d, t[d * N_TILES + i], 0)),
            pl.BlockSpec((1, 1, B_HEADS, B_DK, 128), lambda d, i, t, s, *_: (s[d * N_TILES + i], d, 0, 0, 0)),
        ],
        scratch_shapes=[pltpu.VMEM((B_HEADS, B_DK, 128), F32), pltpu.VMEM((8, 128), F32)]
        + [pltpu.VMEM((TILE, B_W), F32)] * 3 + [pltpu.VMEM((TILE, 128), F32)] * 2,
    )
    return pl.pallas_call(
        _mlstm_kernel,
        grid_spec=grid_spec,
        out_shape=[jax.ShapeDtypeStruct((2, N_TOK, B_W), F32),
                   jax.ShapeDtypeStruct((N_SEQ, 2, B_HEADS, B_DK, 128), F32)],
        compiler_params=_cparams(("arbitrary", "arbitrary")),
        name="mlstm_mixer",
    )(*sched, proj, proj, proj, proj, s0, conv_w, gate_b)


HEAD_G = 128
ATT_SCALE = (C_NOPE + C_ROPE) ** -0.5


def _rope(x, cos, sin_lo, sin_hi):
    return x * cos + pltpu.roll(x, 16, axis=1) * sin_hi + pltpu.roll(x, HEAD_G - 16, axis=1) * sin_lo


def _mla_pre_kernel(xc_ref, qn_ref, kvn_ref, wuq_ref, cos_ref, slo_ref, shi_ref, q_ref, ckv_ref, kpe_ref):
    q_dn = xc_ref[:, 0:C_Q_LORA]
    qn = q_dn * lax.rsqrt(jnp.mean(q_dn * q_dn, axis=-1, keepdims=True) + RMS_EPS) * qn_ref[...]
    q = _dot(qn.astype(BF16), wuq_ref[...])
    cos, slo, shi = cos_ref[...], slo_ref[...], shi_ref[...]
    for h in range(C_HEADS):
        sl = slice(h * HEAD_G, (h + 1) * HEAD_G)
        q_ref[:, sl] = (_rope(q[:, sl], cos, slo, shi) * ATT_SCALE).astype(BF16)
    kv_dn = xc_ref[:, C_Q_LORA:C_Q_LORA + C_KV_LORA]
    ckv_ref[...] = kv_dn * lax.rsqrt(jnp.mean(kv_dn * kv_dn, axis=-1, keepdims=True) + RMS_EPS) * kvn_ref[...]
    kpe_ref[...] = _rope(xc_ref[:, 384:512], cos, slo, shi)


def _mla_pre(proj, q_norm, kv_norm, wuq, rope_tabs):
    cc = COL_C // 512

    def tab_idx(i):
        return (jnp.where(i < N_PROMPT_TILES, TILES_PER_SAMPLE, (i - N_PROMPT_TILES) % TILES_PER_SAMPLE), 0)

    tab_spec = pl.BlockSpec((TILE, HEAD_G), tab_idx)
    return pl.pallas_call(
        _mla_pre_kernel,
        grid=(N_TILES,),
        in_specs=[pl.BlockSpec((TILE, 512), lambda i: (i, cc)),
                  pl.BlockSpec((1, C_Q_LORA), lambda i: (0, 0)),
                  pl.BlockSpec((1, C_KV_LORA), lambda i: (0, 0)),
                  pl.BlockSpec((C_Q_LORA, C_HEADS * HEAD_G), lambda i: (0, 0)),
                  tab_spec, tab_spec, tab_spec],
        out_specs=[pl.BlockSpec((TILE, C_HEADS * HEAD_G), lambda i: (i, 0)),
                   pl.BlockSpec((TILE, C_KV_LORA), lambda i: (i, 0)),
                   pl.BlockSpec((TILE, HEAD_G), lambda i: (i, 0))],
        out_shape=[jax.ShapeDtypeStruct((N_TOK, C_HEADS * HEAD_G), BF16),
                   jax.ShapeDtypeStruct((N_TOK, C_KV_LORA), F32),
                   jax.ShapeDtypeStruct((N_TOK, HEAD_G), F32)],
        compiler_params=_cparams(("parallel",)),
        name="mla_pre",
    )(proj, q_norm, kv_norm, wuq, *rope_tabs)


def _mla_kv_kernel(ckv_ref, kpe_ref, wuk_ref, wuv_ref, k_ref, v_ref):
    ckv = ckv_ref[...].astype(BF16)
    kn = _dot(ckv, wuk_ref[...])
    kpe = kpe_ref[...]
    for h in range(C_HEADS):
        sl = slice(h * HEAD_G, (h + 1) * HEAD_G)
        k_ref[:, sl] = (kn[:, sl] + kpe).astype(BF16)
    v_ref[...] = _dot(ckv, wuv_ref[...]).astype(BF16)


def _mla_kv(ckv_all, kpe_all, wuk, wuv):
    n = ckv_all.shape[0]
    return pl.pallas_call(
        _mla_kv_kernel,
        grid=(n // TILE,),
        in_specs=[pl.BlockSpec((TILE, C_KV_LORA), lambda i: (i, 0)),
                  pl.BlockSpec((TILE, HEAD_G), lambda i: (i, 0)),
                  pl.BlockSpec((C_KV_LORA, C_HEADS * HEAD_G), lambda i: (0, 0)),
                  pl.BlockSpec((C_KV_LORA, C_W), lambda i: (0, 0))],
        out_specs=[pl.BlockSpec((TILE, C_HEADS * HEAD_G), lambda i: (i, 0)),
                   pl.BlockSpec((TILE, C_W), lambda i: (i, 0))],
        out_shape=[jax.ShapeDtypeStruct((n, C_HEADS * HEAD_G), BF16),
                   jax.ShapeDtypeStruct((n, C_W), BF16)],
        compiler_params=_cparams(("parallel",)),
        name="mla_kv",
    )(ckv_all, kpe_all, wuk, wuv)


def _attn_kernel(q_ref, k_ref, v_ref, o_ref):
    for h in range(C_HEADS):
        sl = slice(h * HEAD_G, (h + 1) * HEAD_G)
        s = _dot(q_ref[:, sl], k_ref[0, :, sl], NT)
        e = jnp.exp(s - jnp.max(s, axis=1, keepdims=True))
        den = jnp.sum(e, axis=1, keepdims=True)
        o = _dot(e.astype(BF16), v_ref[0, :, h * C_V:(h + 1) * C_V])
        o_ref[:, h * C_V:(h + 1) * C_V] = o / den


def _attention(q, k, v, tq):
    n_seq, lk, _ = k.shape
    lq = q.shape[0] // n_seq
    nqb = lq // tq
    return pl.pallas_call(
        _attn_kernel,
        grid=(n_seq, nqb),
        in_specs=[pl.BlockSpec((tq, C_HEADS * HEAD_G), lambda s, j: (s * nqb + j, 0)),
                  pl.BlockSpec((1, lk, C_HEADS * HEAD_G), lambda s, j: (s, 0, 0)),
                  pl.BlockSpec((1, lk, C_W), lambda s, j: (s, 0, 0))],
        out_specs=pl.BlockSpec((tq, C_W), lambda s, j: (s * nqb + j, 0)),
        out_shape=jax.ShapeDtypeStruct((q.shape[0], C_W), F32),
        compiler_params=_cparams(("parallel", "parallel")),
        name="mla_attention",
    )(q, k, v)


def _merge_kernel(x_ref, rw0_ref, rw1_ref, ml0_ref, ml1_ref, yc_ref, ga_ref, gb_ref, gc_ref, og_ref,
                  g1_ref, sh2_ref, sc2_ref, rwlw_ref, rwlb_ref, mllw_ref, mllb_ref,
                  pa_ref, pb_ref, pc_ref, wo_ref, l1w_ref, l1b_ref, rtw_ref, rtb_ref,
                  x1_ref, u2_ref, lg_ref):
    ones_blk = _block_ones(A_W, A_HEAD)
    rw0 = rw0_ref[...]
    rw1 = rw1_ref[...]
    ya = _head_norm(rw0[:, 0:256] + rw1[:, 0:256], ones_blk, rwlw_ref[...], rwlb_ref[...], A_GN_EPS)
    ya = (ya + rw0[:, 256:512] + rw1[:, 256:512]) * rw0[:, 512:768]
    yb = _head_norm(ml0_ref[0] + ml1_ref[0], ones_blk, mllw_ref[...], mllb_ref[...], LN_EPS)
    yb = yb * _sigmoid(og_ref[...])
    merged = (_sigmoid(ga_ref[...]) * _dot(ya.astype(BF16), pa_ref[...])
              + _sigmoid(gb_ref[...]) * _dot(yb.astype(BF16), pb_ref[...])
              + _sigmoid(gc_ref[...]) * _dot(yc_ref[...].astype(BF16), pc_ref[...]))
    mix = _dot(merged.astype(BF16), wo_ref[...])
    x1 = _layer_norm(DN_ALPHA * x_ref[...] + g1_ref[0] * mix, l1w_ref[...], l1b_ref[...])
    x1_ref[...] = x1
    u2 = x1 * (1.0 + sc2_ref[0]) + sh2_ref[0]
    u2_ref[...] = u2.astype(BF16)
    lg_ref[...] = _mm(u2, rtw_ref[...], 3) + rtb_ref[...]


def _merge(x, rw, ml, yc, proj, modt, rwlw, rwlb, mllw, mllb, pa, pb, pc, wo, l1w, l1b, rtw, rtb):
    gcol = COL_G // 1024

    def row(shape):
        return pl.BlockSpec(shape, lambda i: (0, 0))

    def mod(kk):
        return pl.BlockSpec((1, 1, D_MODEL), lambda i: (i, 0, kk))

    return pl.pallas_call(
        _merge_kernel,
        grid=(N_TILES,),
        in_specs=[pl.BlockSpec((TILE, D_MODEL), lambda i: (i, 0)),
                  pl.BlockSpec((TILE, 768), lambda i: (i, 0)),
                  pl.BlockSpec((TILE, 768), lambda i: (i, 0)),
                  pl.BlockSpec((1, TILE, B_W), lambda i: (0, i, 0)),
                  pl.BlockSpec((1, TILE, B_W), lambda i: (1, i, 0)),
                  pl.BlockSpec((TILE, C_W), lambda i: (i, 0)),
                  pl.BlockSpec((TILE, D_MODEL), lambda i: (i, gcol)),
                  pl.BlockSpec((TILE, D_MODEL), lambda i: (i, gcol + 1)),
                  pl.BlockSpec((TILE, D_MODEL), lambda i: (i, gcol + 2)),
                  pl.BlockSpec((TILE, B_W), lambda i: (i, (COL_B + 768) // B_W)),
                  mod(2), mod(3), mod(4),
                  row((1, A_W)), row((1, A_W)), row((1, B_W)), row((1, B_W)),
                  row((A_W, D_MODEL)), row((B_W, D_MODEL)), row((C_W, D_MODEL)), row((D_MODEL, D_MODEL)),
                  row((1, D_MODEL)), row((1, D_MODEL)), row((D_MODEL, 128)), row((1, 128))],
        out_specs=[pl.BlockSpec((TILE, D_MODEL), lambda i: (i, 0)),
                   pl.BlockSpec((TILE, D_MODEL), lambda i: (i, 0)),
                   pl.BlockSpec((TILE, 128), lambda i: (i, 0))],
        out_shape=[jax.ShapeDtypeStruct((N_TOK, D_MODEL), F32),
                   jax.ShapeDtypeStruct((N_TOK, D_MODEL), BF16),
                   jax.ShapeDtypeStruct((N_TOK, 128), F32)],
        compiler_params=_cparams(("parallel",)),
        name="merge_postnorm_router",
    )(x, rw[0], rw[1], ml, ml, yc, proj, proj, proj, proj, modt, modt, modt,
      rwlw, rwlb, mllw, mllb, pa, pb, pc, wo, l1w, l1b, rtw, rtb)


N_ASSIGN = N_TOK * TOP_K
N_ROW_BLOCKS = N_ASSIGN // MOE_BLOCK
N_ITEMS = N_ROW_BLOCKS + N_EXPERTS


def _moe_kernel(blk_s, exp_s, lo_s, hi_s, init_s, x_ref, w1g_ref, w1l_ref, b1g_ref, b1l_ref, w2_ref, b2_ref,
                rw_ref, y_ref):
    w = pl.program_id(0)

    @pl.when(init_s[w] == 1)
    def _():
        y_ref[...] = jnp.zeros_like(y_ref)

    @pl.when(hi_s[w] > lo_s[w])
    def _():
        x = x_ref[...]
        hg = jnp.minimum(_dot(x, w1g_ref[0, 0]) + b1g_ref[0, 0], SWIGLU_LIMIT)
        hl = jnp.clip(_dot(x, w1l_ref[0, 0]) + b1l_ref[0, 0], -SWIGLU_LIMIT, SWIGLU_LIMIT)
        act = hg * _sigmoid(SWIGLU_ALPHA * hg) * (hl + 1.0)
        y = _dot(act.astype(BF16), w2_ref[0, 0].astype(BF16)) + b2_ref[0, 0]
        rid = lax.broadcasted_iota(jnp.int32, (MOE_BLOCK, 1), 0)
        mine = (rid >= lo_s[w]) & (rid < hi_s[w])
        y_ref[...] += jnp.where(mine, y * rw_ref[...], 0.0)


def _moe_experts(l, x_sorted, items, w1g, w1l, b1g, b1l, w2, b2, w_sorted):
    def wspec(shape):
        return pl.BlockSpec((1, 1) + shape, lambda i, blk, ex, *_: (l, ex[i], 0, 0))

    grid_spec = pltpu.PrefetchScalarGridSpec(
        num_scalar_prefetch=5,
        grid=(N_ITEMS,),
        in_specs=[pl.BlockSpec((MOE_BLOCK, D_MODEL), lambda i, blk, *_: (blk[i], 0)),
                  wspec((D_MODEL, D_EXPERT)), wspec((D_MODEL, D_EXPERT)),
                  wspec((1, D_EXPERT)), wspec((1, D_EXPERT)),
                  wspec((D_EXPERT, D_MODEL)), wspec((1, D_MODEL)),
                  pl.BlockSpec((MOE_BLOCK, 1), lambda i, blk, *_: (blk[i], 0))],
        out_specs=pl.BlockSpec((MOE_BLOCK, D_MODEL), lambda i, blk, *_: (blk[i], 0)),
    )
    return pl.pallas_call(
        _moe_kernel,
        grid_spec=grid_spec,
        out_shape=jax.ShapeDtypeStruct((N_ASSIGN, D_MODEL), F32),
        compiler_params=_cparams(("arbitrary",)),
        name="moe_experts",
    )(*items, x_sorted, w1g, w1l, b1g, b1l, w2, b2, w_sorted)


def _final_kernel(x_ref, f_ref, g2_ref, w_ref, b_ref, o_ref):
    ffn = (f_ref[0] + f_ref[1]) + (f_ref[2] + f_ref[3])
    o_ref[...] = _layer_norm(DN_ALPHA * x_ref[...] + g2_ref[0] * ffn, w_ref[...], b_ref[...])


def _final_norm(x1, y_slots, modt, w, b):
    return pl.pallas_call(
        _final_kernel,
        grid=(N_TILES,),
        in_specs=[pl.BlockSpec((TILE, D_MODEL), lambda i: (i, 0)),
                  pl.BlockSpec((TOP_K, TILE, D_MODEL), lambda i: (0, i, 0)),
                  pl.BlockSpec((1, 1, D_MODEL), lambda i: (i, 0, 5)),
                  pl.BlockSpec((1, D_MODEL), lambda i: (0, 0)),
                  pl.BlockSpec((1, D_MODEL), lambda i: (0, 0))],
        out_specs=pl.BlockSpec((TILE, D_MODEL), lambda i: (i, 0)),
        out_shape=jax.ShapeDtypeStruct((N_TOK, D_MODEL), F32),
        compiler_params=_cparams(("parallel",)),
        name="ffn_postnorm",
    )(x1, y_slots, modt, w, b)


def _route(logits):
    top_v, top_e = lax.top_k(logits, TOP_K)
    top_w = jax.nn.softmax(top_v, axis=-1)
    flat_e = top_e.reshape(-1).astype(jnp.int32)
    idx = jnp.arange(N_ASSIGN, dtype=jnp.int32)
    _, order, w_sorted = lax.sort((flat_e, idx, top_w.reshape(-1)), num_keys=1)
    _, inv = lax.sort((order, idx), num_keys=1)
    ex = jnp.arange(N_EXPERTS, dtype=jnp.int32)
    counts = jnp.sum((flat_e[:, None] == ex[None, :]).astype(jnp.int32), axis=0)
    end = jnp.cumsum(counts)
    start = end - counts
    first_blk = start // MOE_BLOCK
    n_items = jnp.where(counts > 0, (end - 1) // MOE_BLOCK - first_blk + 1, 0)
    item_end = jnp.cumsum(n_items)
    item_start = item_end - n_items
    w = jnp.arange(N_ITEMS, dtype=jnp.int32)
    valid = w < item_end[-1]
    e_w = jnp.minimum(jnp.sum((item_end[None, :] <= w[:, None]).astype(jnp.int32), axis=1), N_EXPERTS - 1)
    pick = (e_w[:, None] == ex[None, :]).astype(jnp.int32)
    look = lambda tab: jnp.sum(pick * tab[None, :], axis=1)
    blk = jnp.where(valid, look(first_blk) + w - look(item_start), N_ROW_BLOCKS - 1)
    lo = jnp.where(valid, jnp.maximum(look(start) - blk * MOE_BLOCK, 0), 0)
    hi = jnp.where(valid, jnp.minimum(look(end) - blk * MOE_BLOCK, MOE_BLOCK), 0)
    e_last = jnp.max(jnp.where(counts > 0, ex, 0))
    e_w = jnp.where(valid, e_w, e_last)
    init = jnp.concatenate([jnp.ones((1,), jnp.int32), (blk[1:] != blk[:-1]).astype(jnp.int32)])
    items = tuple(a.astype(jnp.int32) for a in (blk, e_w, lo, hi, init))
    return order // TOP_K, w_sorted, inv, items


def _rope_tables():
    rows = DEC_SEQ // GRID_W
    r, col = jnp.meshgrid(jnp.arange(rows, dtype=F32), jnp.arange(GRID_W, dtype=F32), indexing='ij')
    n_freq = C_ROPE // 4
    inv = 1.0 / (ROPE_BASE ** (jnp.arange(n_freq, dtype=F32) / n_freq))
    ang = jnp.concatenate([r.reshape(-1, 1) * inv, col.reshape(-1, 1) * inv], axis=-1)
    cos, sin = jnp.cos(ang), jnp.sin(ang)
    half = C_ROPE // 2
    one = jnp.ones((DEC_SEQ, C_NOPE), F32)
    zero = jnp.zeros((DEC_SEQ, C_NOPE), F32)
    tail1 = jnp.ones((DEC_SEQ, HEAD_G - C_NOPE - C_ROPE), F32)
    tail0 = jnp.zeros((DEC_SEQ, HEAD_G - C_NOPE - C_ROPE), F32)
    zh = jnp.zeros((DEC_SEQ, half), F32)
    t_cos = jnp.concatenate([one, cos, cos, tail1], axis=1)
    t_lo = jnp.concatenate([zero, -sin, zh, tail0], axis=1)
    t_hi = jnp.concatenate([zero, zh, sin, tail0], axis=1)
    ident = jnp.ones((TILE, HEAD_G), F32)
    nil = jnp.zeros((TILE, HEAD_G), F32)
    return (jnp.concatenate([t_cos, ident]), jnp.concatenate([t_lo, nil]), jnp.concatenate([t_hi, nil]))


def _pad_heads(w, n_heads, width):
    lead = w.shape[:-1]
    w = w.reshape(lead + (n_heads, width))
    w = jnp.pad(w, [(0, 0)] * len(lead) + [(0, 0), (0, HEAD_G - width)])
    return w.reshape(lead + (n_heads * HEAD_G,))


def kernel(x_prompt, x_sample, state_rwkv, state_mlstm_c, state_mlstm_n, state_mlstm_m, cache_mla_ckv,
           cache_mla_kpe, c, c_ctx, ada_w, ada_b, w_in, rw_mu, rw_w0, rw_w2, rw_a0, rw_a2, rw_g2, rw_kk,
           rw_ka, rw_rk, rw_ln_w, rw_ln_b, ml_conv, ml_gate_b, ml_ln_w, ml_ln_b, mla_q_norm, mla_wuq,
           mla_kv_norm, mla_wuk, mla_wuv, proj_a, proj_b, proj_c, w_out, ln1_w, ln1_b, router_w, router_b,
           moe_w1, moe_b1, moe_w2, moe_b2, ln2_w, ln2_b):
    L = DEPTH
    x = jnp.concatenate([x_prompt.reshape(-1, D_MODEL), x_sample.reshape(-1, D_MODEL)], axis=0)

    cond = jnp.concatenate([c_ctx[None], c, jnp.zeros((8 - 1 - DEC_BATCH, D_MODEL), F32)], axis=0)
    wa = w_in[:, :, 0:A_IN]
    wb = w_in[:, :, A_IN:A_IN + 1024]
    wbg = w_in[:, :, A_IN + 1024:A_IN + B_IN]
    wc = w_in[:, :, A_IN + B_IN:A_IN + B_IN + C_IN]
    wg = w_in[:, :, A_IN + B_IN + C_IN:]
    z = lambda n: jnp.zeros((L, D_MODEL, n), F32)
    w_proj = jnp.concatenate([
        wa, wg, wb,
        wc[:, :, 0:384], z(64), wc[:, :, 384:416], z(32),
        wbg[:, :, 0:8], z(120), wbg[:, :, 8:16], z(120)], axis=2).astype(BF16)
    gate_b = jnp.pad(ml_gate_b.reshape(L, 2, 1, 2 * B_HEADS), ((0, 0), (0, 0), (0, 0), (0, 128 - 2 * B_HEADS)))
    wuq = _pad_heads(mla_wuq, C_HEADS, C_NOPE + C_ROPE).astype(BF16)
    wuk = _pad_heads(mla_wuk, C_HEADS, C_NOPE).astype(BF16)
    wuv = mla_wuv.astype(BF16)
    pa, pb, pc, wo = proj_a.astype(BF16), proj_b.astype(BF16), proj_c.astype(BF16), w_out.astype(BF16)
    rtw = jnp.pad(router_w, ((0, 0), (0, 0), (0, 128 - N_EXPERTS)))
    rtb = jnp.pad(router_b, ((0, 0), (0, 128 - N_EXPERTS))).reshape(L, 1, 128)
    w1 = moe_w1.reshape(L, N_EXPERTS, D_MODEL, D_EXPERT, 2)
    w1g, w1l = w1[..., 0].astype(BF16), w1[..., 1].astype(BF16)
    b1 = moe_b1.reshape(L, N_EXPERTS, 1, D_EXPERT, 2)
    b1g, b1l = b1[..., 0], b1[..., 1]
    b2 = moe_b2.reshape(L, N_EXPERTS, 1, D_MODEL)
    rope_tabs = _rope_tables()
    row2 = lambda a: a.reshape(L, 1, -1)

    rw_s0 = jnp.concatenate([jnp.zeros((BATCH, L, 2, A_HEADS, A_HEAD, A_HEAD), F32),
                             jnp.swapaxes(state_rwkv, -1, -2)], axis=0)
    m_col = jnp.broadcast_to(state_mlstm_m[..., None, None], state_mlstm_m.shape + (B_DK, 1))
    ml_dec = jnp.concatenate([state_mlstm_c, state_mlstm_n[..., None], m_col,
                              jnp.zeros(state_mlstm_m.shape + (B_DK, 128 - B_DK - 2), F32)], axis=-1)
    ml_s0 = jnp.concatenate([jnp.zeros((BATCH,) + ml_dec.shape[1:], F32), ml_dec], axis=0)
    kpe_cache = jnp.pad(cache_mla_kpe, ((0, 0), (0, 0), (0, 0), (C_NOPE, HEAD_G - C_NOPE - C_ROPE)))

    mod = _modulation(cond, ada_w, ada_b)
    n_prompt = BATCH * SEQ
    outs = {k: [] for k in ('rw', 'mlc', 'ckv', 'kpe')}
    for l in range(L):
        modt = mod[l][_MOD_ROW_OF_TILE].reshape(N_TILES, 1, 6 * D_MODEL)
        proj = _in_projection(x, modt, w_proj[l])
        rw, rw_fin = _rwkv(proj, rw_s0[:, l], row2(rw_mu)[l], rw_w0[l][:, None], rw_w2[l], rw_a0[l][:, None],
                           rw_a2[l], rw_g2[l], row2(rw_kk)[l], row2(rw_ka)[l], row2(rw_rk)[l])
        ml, ml_fin = _mlstm(proj, ml_s0[:, l], ml_conv[l], gate_b[l])
        q, ckv, kpe = _mla_pre(proj, row2(mla_q_norm)[l], row2(mla_kv_norm)[l], wuq[l], rope_tabs)
        ckv_all = jnp.concatenate([ckv[:n_prompt]] + [
            t for s in range(DEC_BATCH)
            for t in (cache_mla_ckv[s, l], ckv[n_prompt + s * DEC_SEQ:n_prompt + (s + 1) * DEC_SEQ])], axis=0)
        kpe_all = jnp.concatenate([kpe[:n_prompt]] + [
            t for s in range(DEC_BATCH)
            for t in (kpe_cache[s, l], kpe[n_prompt + s * DEC_SEQ:n_prompt + (s + 1) * DEC_SEQ])], axis=0)
        kf, vf = _mla_kv(ckv_all, kpe_all, wuk[l], wuv[l])
        lk = PAST_LEN + DEC_SEQ
        yc_p = _attention(q[:n_prompt], kf[:n_prompt].reshape(BATCH, SEQ, -1),
                          vf[:n_prompt].reshape(BATCH, SEQ, -1), SEQ)
        yc_s = _attention(q[n_prompt:], kf[n_prompt:].reshape(DEC_BATCH, lk, -1),
                          vf[n_prompt:].reshape(DEC_BATCH, lk, -1), 256)
        yc = jnp.concatenate([yc_p, yc_s], axis=0)
        x1, u2, logits = _merge(x, rw, ml, yc, proj, modt, row2(rw_ln_w)[l], row2(rw_ln_b)[l],
                                row2(ml_ln_w)[l], row2(ml_ln_b)[l], pa[l], pb[l], pc[l], wo[l],
                                row2(ln1_w)[l], row2(ln1_b)[l], rtw[l], rtb[l])
        tok_sorted, w_sorted, inv, items = _route(logits[:, :N_EXPERTS])
        y_sorted = _moe_experts(l, u2[tok_sorted], items, w1g, w1l, b1g, b1l, moe_w2, b2, w_sorted[:, None])
        y_slots = y_sorted[inv.reshape(N_TOK, TOP_K).T.reshape(-1)].reshape(TOP_K, N_TOK, D_MODEL)
        x = _final_norm(x1, y_slots, modt, row2(ln2_w)[l], row2(ln2_b)[l])
        outs['rw'].append(jnp.swapaxes(jnp.stack([rw_fin[0][:BATCH], rw_fin[1][:BATCH]], axis=1), -1, -2))
        outs['mlc'].append(ml_fin[:BATCH])
        outs['ckv'].append(ckv[:n_prompt].reshape(BATCH, SEQ, C_KV_LORA))
        outs['kpe'].append(kpe[:n_prompt, C_NOPE:C_NOPE + C_ROPE].reshape(BATCH, SEQ, C_ROPE))

    stack = lambda k: jnp.stack(outs[k], axis=1)
    mlc = stack('mlc')
    return (x[:n_prompt].reshape(BATCH, SEQ, D_MODEL), x[n_prompt:].reshape(DEC_BATCH, DEC_SEQ, D_MODEL),
            stack('rw'), mlc[..., 0:B_DK], mlc[..., N_COL], mlc[..., 0, M_COL], stack('ckv'), stack('kpe'))
```

```python
import functools

import numpy as np
import jax
import jax.numpy as jnp
from jax import lax
from jax.experimental import pallas as pl
from jax.experimental.pallas import tpu as pltpu

F32 = jnp.float32
BF16 = jnp.bfloat16

D_MODEL = 1024
BATCH = 16
SEQ = 256
DEPTH = 4
DEC_BATCH = 2
DEC_SEQ = 4096
PAST_LEN = 256
GRID_W = 64

A_HEADS = 4
A_HEAD = 64
A_W = 256
A_GN_EPS = 64e-5
B_HEADS = 4
B_DK = 64
B_W = 256
C_HEADS = 8
C_NOPE = 64
C_ROPE = 32
C_V = 64
C_Q_LORA = 256
C_KV_LORA = 128
C_W = 512
ROPE_BASE = 10000.0
N_EXPERTS = 32
TOP_K = 4
D_EXPERT = 1024
SWIGLU_LIMIT = 7.0
SWIGLU_ALPHA = 1.702
MOE_BLOCK = 256
A_IN = 1024
B_IN = 1040
C_IN = 416
DN_ALPHA = (2 * DEPTH) ** 0.25
LN_EPS = 1e-5
RMS_EPS = 1e-6

TILE = 256
CHUNK = 64
N_TOK = BATCH * SEQ + DEC_BATCH * DEC_SEQ
N_TILES = N_TOK // TILE
N_PROMPT_TILES = BATCH * SEQ // TILE
TILES_PER_SAMPLE = DEC_SEQ // TILE
N_SEQ = BATCH + DEC_BATCH
HALO = 8

COL_A = 0
COL_G = 1024
COL_B = 4096
COL_C = 5120
COL_BG = 5632
N_PROJ = 5888

VMEM_LIMIT = 56 * 1024 * 1024


def _cparams(sem):
    return pltpu.CompilerParams(dimension_semantics=sem, vmem_limit_bytes=VMEM_LIMIT)


NN = ((1,), (0,))
NT = ((1,), (1,))
TN = ((0,), (0,))


def _dot(a, b, dims=NN):
    return lax.dot_general(a, b, (dims, ((), ())), preferred_element_type=F32)


def _split(x, n):
    parts, r = [], x
    for i in range(n):
        p = r.astype(BF16)
        parts.append(p)
        if i + 1 < n:
            r = r - p.astype(F32)
    return parts


def _mm(a, b, passes=1, dims=NN):
    if passes == 1:
        return _dot(a.astype(BF16), b.astype(BF16), dims)
    ah, al = _split(a, 2)
    bh, bl = _split(b, 2)
    return _dot(ah, bh, dims) + (_dot(al, bh, dims) + _dot(ah, bl, dims))


def _mm_exact_l(t01, x):
    x0, x1, x2 = _split(x, 3)
    return _dot(t01, x0) + (_dot(t01, x1) + _dot(t01, x2))


def _mm_exact_r(x, t01):
    x0, x1, x2 = _split(x, 3)
    return _dot(x0, t01) + (_dot(x1, t01) + _dot(x2, t01))


def _sigmoid(x):
    return 1.0 / (1.0 + jnp.exp(-x))


def _softplus(x):
    return jnp.maximum(x, 0.0) + jnp.log(1.0 + jnp.exp(-jnp.abs(x)))


def _block_ones(n, blk):
    r = lax.broadcasted_iota(jnp.int32, (n, n), 0) // blk
    c = lax.broadcasted_iota(jnp.int32, (n, n), 1) // blk
    return jnp.where(r == c, 1.0, 0.0).astype(BF16)


def _layer_norm(x, w, b):
    mu = jnp.mean(x, axis=-1, keepdims=True)
    xc = x - mu
    var = jnp.mean(xc * xc, axis=-1, keepdims=True)
    return xc * lax.rsqrt(var + LN_EPS) * w + b


def _head_norm(x, ones_blk, w, b, eps):
    mu = _mm_exact_r(x, ones_blk) * (1.0 / 64.0)
    xc = x - mu
    var = _mm_exact_r(xc * xc, ones_blk) * (1.0 / 64.0)
    return xc * lax.rsqrt(var + eps) * w + b


def _shifted(x, prev_row, next_row):
    rid = lax.broadcasted_iota(jnp.int32, x.shape, 0)
    xp = jnp.where(rid == 0, prev_row, pltpu.roll(x, 1, axis=0))
    xn = jnp.where(rid == x.shape[0] - 1, next_row, pltpu.roll(x, x.shape[0] - 1, axis=0))
    return xp, xn


def _schedule():
    seq_of_tile = np.concatenate([np.arange(BATCH), BATCH + np.repeat(np.arange(DEC_BATCH), TILES_PER_SAMPLE)])
    first_of_tile = np.ones(N_TILES, np.int32)
    last_of_tile = np.ones(N_TILES, np.int32)
    for s in range(DEC_BATCH):
        base = N_PROMPT_TILES + s * TILES_PER_SAMPLE
        first_of_tile[base + 1: base + TILES_PER_SAMPLE] = 0
        last_of_tile[base: base + TILES_PER_SAMPLE - 1] = 0
    tile_fwd = np.arange(N_TILES)
    tile_bwd = np.arange(N_TILES)
    for s in range(DEC_BATCH):
        base = N_PROMPT_TILES + s * TILES_PER_SAMPLE
        tile_bwd[base: base + TILES_PER_SAMPLE] = base + TILES_PER_SAMPLE - 1 - np.arange(TILES_PER_SAMPLE)
    tile = np.concatenate([tile_fwd, tile_bwd]).astype(np.int32)
    seq = seq_of_tile[tile].astype(np.int32)
    begins = np.concatenate([first_of_tile[tile_fwd], last_of_tile[tile_bwd]]).astype(np.int32)
    ends = np.concatenate([last_of_tile[tile_fwd], first_of_tile[tile_bwd]]).astype(np.int32)
    has_prev = (1 - first_of_tile).astype(np.int32)
    has_next = (1 - last_of_tile).astype(np.int32)
    return tile, seq, begins, ends, has_prev, has_next


_SCHED = _schedule()
_SEQ_OF_TILE = np.concatenate([np.arange(BATCH), BATCH + np.repeat(np.arange(DEC_BATCH), TILES_PER_SAMPLE)]).astype(np.int32)
_MOD_ROW_OF_TILE = np.concatenate([np.zeros(N_PROMPT_TILES), 1 + np.repeat(np.arange(DEC_BATCH), TILES_PER_SAMPLE)]).astype(np.int32)


def _direction_masks(d):
    row = lax.broadcasted_iota(jnp.int32, (CHUNK, CHUNK), 0)
    col = lax.broadcasted_iota(jnp.int32, (CHUNK, CHUNK), 1)
    diff = (row - col) * (1 - 2 * d)
    return diff >= 0, diff > 0, diff <= 0


def _mod_kernel(c_ref, w_ref, b_ref, o_ref):
    c = c_ref[...]
    o_ref[0] = _mm(c * _sigmoid(c), w_ref[0], 3) + b_ref[0]


def _modulation(cond, ada_w, ada_b):
    tn = 1536
    return pl.pallas_call(
        _mod_kernel,
        grid=(DEPTH, 6 * D_MODEL // tn),
        in_specs=[pl.BlockSpec((8, D_MODEL), lambda l, j: (0, 0)),
                  pl.BlockSpec((1, D_MODEL, tn), lambda l, j: (l, 0, j)),
                  pl.BlockSpec((1, 1, tn), lambda l, j: (l, 0, j))],
        out_specs=pl.BlockSpec((1, 8, tn), lambda l, j: (l, 0, j)),
        out_shape=jax.ShapeDtypeStruct((DEPTH, 8, 6 * D_MODEL), F32),
        compiler_params=_cparams(("parallel", "parallel")),
        name="ada_modulation",
    )(cond, ada_w, ada_b.reshape(DEPTH, 1, 6 * D_MODEL))


IN_TM = 512
IN_TN = N_PROJ // 2


def _inproj_kernel(x_ref, sh_ref, sc_ref, w_ref, o_ref):
    u = x_ref[...] * (1.0 + sc_ref[0]) + sh_ref[0]
    o_ref[...] = _dot(u.astype(BF16), w_ref[...])


def _in_projection(x, modt, w):
    rep = IN_TM // TILE
    return pl.pallas_call(
        _inproj_kernel,
        grid=(N_PROJ // IN_TN, N_TOK // IN_TM),
        in_specs=[pl.BlockSpec((IN_TM, D_MODEL), lambda j, i: (i, 0)),
                  pl.BlockSpec((1, 1, D_MODEL), lambda j, i: (rep * i, 0, 0)),
                  pl.BlockSpec((1, 1, D_MODEL), lambda j, i: (rep * i, 0, 1)),
                  pl.BlockSpec((D_MODEL, IN_TN), lambda j, i: (0, j))],
        out_specs=pl.BlockSpec((IN_TM, IN_TN), lambda j, i: (i, j)),
        out_shape=jax.ShapeDtypeStruct((N_TOK, N_PROJ), F32),
        compiler_params=_cparams(("parallel", "parallel")),
        name="in_projection",
    )(x, modt, modt, w)


RW_P = 1
RW_GRAM_P = 1
RW_UPD_P = 3
RW_INV_P = 1
RW_ST_P = 3
HB = A_HEADS * CHUNK


def _pre(x, passes):
    return tuple(_split(x, 2)) if passes == 3 else (x.astype(BF16),)


def _mmp(a, b, dims=NN):
    out = _dot(a[0], b[0], dims)
    if len(a) == 2 and len(b) == 2:
        out = out + (_dot(a[1], b[0], dims) + _dot(a[0], b[1], dims))
    return out


def _head_expand(m, same_head):
    return jnp.where(same_head, jnp.concatenate([m] * A_HEADS, axis=0), 0.0)


def _head_stack(m):
    return jnp.concatenate([m[:, h * A_HEAD:(h + 1) * A_HEAD] for h in range(A_HEADS)], axis=0)


def _head_unstack(m):
    return jnp.concatenate([m[h * CHUNK:(h + 1) * CHUNK, :] for h in range(A_HEADS)], axis=1)


def _rwkv_kernel(d, tile_s, seq_s, begin_s, end_s, hasprev_s, hasnext_s,
                 xa_ref, xp_ref, xn_ref, s0_ref, mu_ref, w0_ref, w2_ref, a0_ref, a2_ref, g2_ref,
                 kk_ref, ka_ref, rk_ref,
                 out_ref, sfin_ref,
                 st_scr):
    step = pl.program_id(0)
    tile = tile_s[step]

    @pl.when(begin_s[step] == 1)
    def _():
        st_scr[...] = s0_ref[0]

    x = xa_ref[...]
    prev_row = xp_ref[HALO - 1:HALO, :] * hasprev_s[tile].astype(F32)
    next_row = xn_ref[0:1, :] * hasnext_s[tile].astype(F32)
    xp, xn = _shifted(x, prev_row, next_row)
    xs = x + mu_ref[...] * (0.5 * (xp + xn) - x)
    r = xs[:, 0:256]
    k = xs[:, 256:512]
    v = xs[:, 512:768]
    wl = xs[:, 768:832]
    al = xs[:, 832:896]
    gl = xs[:, 896:1024]

    ones_blk = _block_ones(A_W, A_HEAD)
    kkv = k * kk_ref[...]
    nrm = jnp.sqrt(_mm_exact_r(kkv * kkv, ones_blk))
    kap = kkv / jnp.maximum(nrm, 1e-6)
    wpre = w0_ref[...] + _mm(jnp.tanh(wl), w2_ref[...], 3)
    lw = -jnp.exp(-_softplus(-wpre) - 0.5)
    a = _sigmoid(a0_ref[...] + _mm(al, a2_ref[...], 3))
    kd = k * (1.0 + (a - 1.0) * ka_ref[...])
    bonus = _mm_exact_r(r * kd * rk_ref[...], ones_blk) * v
    out_ref[:, 256:512] = bonus
    out_ref[:, 512:768] = _mm(_sigmoid(gl), g2_ref[...], 3)

    b = kap * a

    incl64, _, _ = _direction_masks(d)
    tinc = jnp.where(incl64, 1.0, 0.0).astype(BF16)
    row = lax.broadcasted_iota(jnp.int32, (HB, HB), 0)
    col = lax.broadcasted_iota(jnp.int32, (HB, HB), 1)
    same_head = (row // CHUNK) == (col // CHUNK)
    order = jnp.where(same_head, (row % CHUNK - col % CHUNK) * (1 - 2 * d), -1)
    incl = order >= 0
    strict = order > 0
    eye = jnp.where(row == col, 1.0, 0.0)
    xor = row ^ col
    level_masks = [(xor >> s) == 1 for s in range(CHUNK.bit_length() - 1)]
    n_chunks = TILE // CHUNK
    chunks = [c if d == 0 else n_chunks - 1 - c for c in range(n_chunks)]
    sl = [slice(c * CHUNK, (c + 1) * CHUNK) for c in chunks]
    rng = range(n_chunks)

    lwc = [lw[s] for s in sl]
    cs = [_mm_exact_l(tinc, x) for x in lwc]
    w_in = [jnp.exp(x) for x in cs]
    w_ex = [jnp.exp(cs[i] - lwc[i]) for i in rng]
    w_inv = [jnp.exp(-x) for x in cs]
    w_tot = [jnp.exp(jnp.sum(x, axis=0, keepdims=True)) for x in lwc]
    kt = [_head_expand(kap[sl[i]] * w_ex[i], same_head) for i in rng]
    rt = [_head_expand(r[sl[i]] * w_in[i], same_head) for i in rng]
    bt = [b[sl[i]] * w_inv[i] for i in rng]
    kdt = [kd[sl[i]] * w_inv[i] for i in rng]
    v_stack = [_head_stack(v[s]) for s in sl]
    v_st = [_pre(x, RW_P) for x in v_stack]
    v_su = v_st if RW_UPD_P == RW_P else [_pre(x, RW_UPD_P) for x in v_stack]
    gram = [_mm(jnp.concatenate([kt[i], rt[i]], axis=0),
                jnp.concatenate([bt[i]] * A_HEADS + [kdt[i]] * A_HEADS, axis=0), RW_GRAM_P, NT) for i in rng]
    l_b = [jnp.where(strict, g[:HB, :HB], 0.0) for g in gram]
    l_k = [jnp.where(strict, g[:HB, HB:], 0.0) for g in gram]
    m_b = [_pre(jnp.where(incl, g[HB:, :HB], 0.0), RW_P) for g in gram]
    m_k = [jnp.where(incl, g[HB:, HB:], 0.0) for g in gram]
    t_inv = [eye - jnp.where(level_masks[0], x, 0.0) for x in l_b]
    for mask in level_masks[1:]:
        tp = [_pre(t, RW_INV_P) for t in t_inv]
        w1 = [_mmp(tp[i], _pre(jnp.where(mask, l_b[i], 0.0), RW_INV_P)) for i in rng]
        t_inv = [t_inv[i] - _mmp(_pre(w1[i], RW_INV_P), tp[i]) for i in rng]
    lkv = [_mmp(_pre(l_k[i], RW_P), v_st[i]) for i in rng]
    xx = [_mm(t_inv[i], jnp.concatenate([kt[i], lkv[i]], axis=1), RW_P) for i in rng]
    xxp = [_pre(x, RW_P) for x in xx]
    mx = [_mmp(m_b[i], xxp[i]) for i in rng]
    rhat = [_pre(rt[i] - mx[i][:, :HB], RW_ST_P) for i in rng]
    y0 = [_mmp(_pre(m_k[i], RW_P), v_st[i]) - mx[i][:, HB:] for i in rng]
    xxu = xxp if RW_UPD_P == RW_P else [_pre(x, RW_UPD_P) for x in xx]
    bhp = [_pre(_head_expand(bt[i] * w_tot[i], same_head), RW_UPD_P) for i in rng]
    bx = [_mmp(bhp[i], xxu[i], TN) for i in rng]
    g_mat = [_pre(eye * w_tot[i] - bx[i][:, :HB], RW_ST_P) for i in rng]
    h_mat = [_mmp(_pre(_head_expand(kdt[i] * w_tot[i], same_head), RW_UPD_P), v_su[i], TN) - bx[i][:, HB:]
             for i in rng]

    st = st_scr[...]
    for i in rng:
        stp = _pre(st, RW_ST_P)
        out_ref[sl[i], 0:256] = _head_unstack(_mmp(rhat[i], stp) + y0[i])
        st = _mmp(g_mat[i], stp) + h_mat[i]
    st_scr[...] = st

    @pl.when(end_s[step] == 1)
    def _():
        sfin_ref[0] = st


def _rwkv_direction(d, proj, s0, mu, w0, w2, a0, a2, g2, kk, ka, rk):
    nrb = N_TOK // HALO
    per = TILE // HALO
    tile, seq, begins, ends, has_prev, has_next = _SCHED
    half = slice(d * N_TILES, (d + 1) * N_TILES)
    sched = tuple(jnp.asarray(a) for a in (tile[half], seq[half], begins[half], ends[half], has_prev, has_next))

    def const(shape):
        return pl.BlockSpec(shape, lambda i, *_: (0,) * len(shape))

    grid_spec = pltpu.PrefetchScalarGridSpec(
        num_scalar_prefetch=6,
        grid=(N_TILES,),
        in_specs=[
            pl.BlockSpec((TILE, A_IN), lambda i, t, *_: (t[i], COL_A // A_IN)),
            pl.BlockSpec((HALO, A_IN), lambda i, t, *_: (jnp.maximum(t[i] * per - 1, 0), 0)),
            pl.BlockSpec((HALO, A_IN), lambda i, t, *_: (jnp.minimum(t[i] * per + per, nrb - 1), 0)),
            pl.BlockSpec((1, HB, A_HEAD), lambda i, t, s, *_: (s[i], 0, 0)),
            const((1, A_IN)),
            const((1, A_W)), const((64, A_W)), const((1, A_W)), const((64, A_W)),
            const((128, A_W)), const((1, A_W)), const((1, A_W)), const((1, A_W)),
        ],
        out_specs=[
            pl.BlockSpec((TILE, 768), lambda i, t, *_: (t[i], 0)),
            pl.BlockSpec((1, HB, A_HEAD), lambda i, t, s, *_: (s[i], 0, 0)),
        ],
        scratch_shapes=[pltpu.VMEM((HB, A_HEAD), F32)],
    )
    return pl.pallas_call(
        functools.partial(_rwkv_kernel, d),
        grid_spec=grid_spec,
        out_shape=[jax.ShapeDtypeStruct((N_TOK, 768), F32),
                   jax.ShapeDtypeStruct((N_SEQ, HB, A_HEAD), F32)],
        compiler_params=_cparams(("arbitrary",)),
        name="rwkv7_mixer_fwd" if d == 0 else "rwkv7_mixer_bwd",
    )(*sched, proj, proj, proj, s0, mu, w0, w2, a0, a2, g2, kk, ka, rk)


def _rwkv(proj, s0, mu, w0, w2, a0, a2, g2, kk, ka, rk):
    outs = [_rwkv_direction(d, proj, s0[:, d].reshape(N_SEQ, HB, A_HEAD), mu, w0[d], w2[d], a0[d], a2[d],
                            g2, kk, ka, rk) for d in range(2)]
    fin = tuple(o[1].reshape(N_SEQ, A_HEADS, A_HEAD, A_HEAD) for o in outs)
    return (outs[0][0], outs[1][0]), fin


ML_P = 1
N_COL = B_DK
M_COL = B_DK + 1
NEG = -1e30


def _per_head_col(x, first):
    return jnp.concatenate([x[:, first + h:first + h + 1] for h in range(B_HEADS)], axis=0)


def _per_head_row(x_t, first):
    return jnp.concatenate([x_t[first + h:first + h + 1, :] for h in range(B_HEADS)], axis=1)


def _head_fill(x, first):
    return jnp.concatenate([jnp.broadcast_to(x[:, first + h:first + h + 1], (CHUNK, 1)) for h in range(B_HEADS)],
                           axis=0)


def _head_max(x):
    return jnp.concatenate(
        [jnp.broadcast_to(jnp.max(x[h * CHUNK:(h + 1) * CHUNK], axis=0, keepdims=True), (CHUNK, 1))
         for h in range(B_HEADS)], axis=0)


def _mlstm_kernel(d, tile_s, seq_s, begin_s, end_s, hasprev_s, hasnext_s,
                  xb_ref, xp_ref, xn_ref, gt_ref, s0_ref, cw_ref, gb_ref,
                  out_ref, sfin_ref,
                  cn_scr, m_scr):
    step = pl.program_id(0)
    tile = tile_s[step]

    @pl.when(begin_s[step] == 1)
    def _():
        s0 = s0_ref[0]
        cn_scr[...] = s0
        m_scr[...] = s0[:, M_COL:M_COL + 1]

    qk = xb_ref[:, 0:512]
    prev_row = xp_ref[HALO - 1:HALO, 0:512] * hasprev_s[tile].astype(F32)
    next_row = xn_ref[0:1, 0:512] * hasnext_s[tile].astype(F32)
    qp, qn = _shifted(qk, prev_row, next_row)
    conv = cw_ref[0:1, :] * qp + cw_ref[1:2, :] * qk + cw_ref[2:3, :] * qn
    act = conv * _sigmoid(conv)
    q = act[:, 0:256]
    k = act[:, 256:512] * (B_DK ** -0.5)
    v = xb_ref[:, 512:768]
    li = gt_ref[...] + gb_ref[...]
    lf = -_softplus(-li)

    incl64, _, incl64_t = _direction_masks(d)
    tinc = jnp.where(incl64, 1.0, 0.0).astype(BF16)
    tinc_t = jnp.where(incl64_t, 1.0, 0.0).astype(BF16)
    row = lax.broadcasted_iota(jnp.int32, (HB, HB), 0)
    col = lax.broadcasted_iota(jnp.int32, (HB, HB), 1)
    same_head = (row // CHUNK) == (col // CHUNK)
    causal = jnp.where(same_head, (row % CHUNK - col % CHUNK) * (1 - 2 * d), -1) >= 0
    lane = lax.broadcasted_iota(jnp.int32, (HB, 128), 1)
    n_chunks = TILE // CHUNK
    chunks = [c if d == 0 else n_chunks - 1 - c for c in range(n_chunks)]
    sl = [slice(c * CHUNK, (c + 1) * CHUNK) for c in chunks]
    rng = range(n_chunks)
    fg = B_HEADS

    li_c = [li[s] for s in sl]
    lf_c = [lf[s] for s in sl]
    b_c = [_mm_exact_l(tinc, x) for x in lf_c]
    b_r = [_mm_exact_r(x.T, tinc_t) for x in lf_c]
    li_r = [x.T for x in li_c]
    bc = [_per_head_col(x, fg) for x in b_c]
    lic = [_per_head_col(x, 0) for x in li_c]
    br = [_per_head_row(x, fg) for x in b_r]
    lir = [_per_head_row(x, 0) for x in li_r]
    be = [_head_fill(jnp.sum(x, axis=0, keepdims=True), fg) for x in lf_c]
    wlog = [be[i] - bc[i] + lic[i] for i in rng]
    wmax = [_head_max(x) for x in wlog]
    m_old, m_new = [], []
    m = m_scr[...]
    for i in rng:
        m_old.append(m)
        m = jnp.maximum(be[i] + m, wmax[i])
        m_new.append(m)
    m_scr[...] = m

    dlog = [jnp.where(causal, bc[i] - br[i] + lir[i], NEG) for i in rng]
    inter = [bc[i] + m_old[i] for i in rng]
    mj = [jnp.maximum(jnp.max(dlog[i], axis=1, keepdims=True), inter[i]) for i in rng]
    q_exp = [_pre(_head_expand(q[s], same_head), ML_P) for s in sl]
    k_exp = [_head_expand(k[s], same_head) for s in sl]
    s_mat = [_mmp(q_exp[i], _pre(jnp.concatenate([k[sl[i]]] * B_HEADS, axis=0), ML_P), NT)
             * jnp.exp(dlog[i] - mj[i]) for i in rng]
    e_int = [jnp.exp(inter[i] - mj[i]) for i in rng]
    v_aug = [jnp.where(lane == N_COL, 1.0,
                       jnp.concatenate([_head_stack(v[s]), jnp.zeros((HB, 128 - B_DK), F32)], axis=1)) for s in sl]
    sv = [_mm(s_mat[i], v_aug[i], ML_P) for i in rng]
    wk = [jnp.exp(wlog[i] - m_new[i]) for i in rng]
    dec = [jnp.exp(be[i] + m_old[i] - m_new[i]) for i in rng]
    kv = [_mm(k_exp[i], wk[i] * v_aug[i], ML_P, TN) for i in rng]
    floor = [jnp.exp(-x) for x in mj]

    cn = cn_scr[...]
    for i in rng:
        nd = sv[i] + e_int[i] * _mmp(q_exp[i], _pre(cn, ML_P))
        den = jnp.maximum(jnp.abs(nd[:, N_COL:N_COL + 1]), floor[i])
        out_ref[sl[i], :] = _head_unstack(nd[:, 0:B_DK] / den)
        cn = dec[i] * cn + kv[i]
    cn_scr[...] = cn

    @pl.when(end_s[step] == 1)
    def _():
        sfin_ref[0] = jnp.where(lane == M_COL, m, cn)


def _mlstm_direction(d, proj, s0, conv_w, gate_b):
    nrb = N_TOK // HALO
    per = TILE // HALO
    tile, seq, begins, ends, has_prev, has_next = _SCHED
    half = slice(d * N_TILES, (d + 1) * N_TILES)
    sched = tuple(jnp.asarray(a) for a in (tile[half], seq[half], begins[half], ends[half], has_prev, has_next))
    cb = COL_B // 1024
    grid_spec = pltpu.PrefetchScalarGridSpec(
        num_scalar_prefetch=6,
        grid=(N_TILES,),
        in_specs=[
            pl.BlockSpec((TILE, 1024), lambda i, t, *_: (t[i], cb)),
            pl.BlockSpec((HALO, 1024), lambda i, t, *_: (jnp.maximum(t[i] * per - 1, 0), cb)),
            pl.BlockSpec((HALO, 1024), lambda i, t, *_: (jnp.minimum(t[i] * per + per, nrb - 1), cb)),
            pl.BlockSpec((TILE, 128), lambda i, t, *_: (t[i], COL_BG // 128 + d)),
            pl.BlockSpec((1, HB, 128), lambda i, t, s, *_: (s[i], 0, 0)),
            pl.BlockSpec((3, 512), lambda i, *_: (0, 0)),
            pl.BlockSpec((1, 128), lambda i, *_: (0, 0)),
        ],
        out_specs=[
            pl.BlockSpec((TILE, B_W), lambda i, t, *_: (t[i], 0)),
            pl.BlockSpec((1, HB, 128), lambda i, t, s, *_: (s[i], 0, 0)),
        ],
        scratch_shapes=[pltpu.VMEM((HB, 128), F32), pltpu.VMEM((HB, 1), F32)],
    )
    return pl.pallas_call(
        functools.partial(_mlstm_kernel, d),
        grid_spec=grid_spec,
        out_shape=[jax.ShapeDtypeStruct((N_TOK, B_W), F32),
                   jax.ShapeDtypeStruct((N_SEQ, HB, 128), F32)],
        compiler_params=_cparams(("arbitrary",)),
        name="mlstm_mixer_fwd" if d == 0 else "mlstm_mixer_bwd",
    )(*sched, proj, proj, proj, proj, s0, conv_w, gate_b)


def _mlstm(proj, s0, conv_w, gate_b):
    outs = [_mlstm_direction(d, proj, s0[:, d].reshape(N_SEQ, HB, 128), conv_w, gate_b[d]) for d in range(2)]
    fin = jnp.stack([o[1].reshape(N_SEQ, B_HEADS, B_DK, 128) for o in outs], axis=1)
    return (outs[0][0], outs[1][0]), fin


HEAD_G = 128
ATT_SCALE = (C_NOPE + C_ROPE) ** -0.5


def _rope(x, cos, sin_lo, sin_hi):
    return x * cos + pltpu.roll(x, 16, axis=1) * sin_hi + pltpu.roll(x, HEAD_G - 16, axis=1) * sin_lo


def _mla_pre_kernel(xc_ref, qn_ref, kvn_ref, wuq_ref, cos_ref, slo_ref, shi_ref, q_ref, ckv_ref, kpe_ref):
    q_dn = xc_ref[:, 0:C_Q_LORA]
    qn = q_dn * lax.rsqrt(jnp.mean(q_dn * q_dn, axis=-1, keepdims=True) + RMS_EPS) * qn_ref[...]
    q = _dot(qn.astype(BF16), wuq_ref[...])
    cos, slo, shi = cos_ref[...], slo_ref[...], shi_ref[...]
    for h in range(C_HEADS):
        sl = slice(h * HEAD_G, (h + 1) * HEAD_G)
        q_ref[:, sl] = (_rope(q[:, sl], cos, slo, shi) * ATT_SCALE).astype(BF16)
    kv_dn = xc_ref[:, C_Q_LORA:C_Q_LORA + C_KV_LORA]
    ckv_ref[...] = kv_dn * lax.rsqrt(jnp.mean(kv_dn * kv_dn, axis=-1, keepdims=True) + RMS_EPS) * kvn_ref[...]
    kpe_ref[...] = _rope(xc_ref[:, 384:512], cos, slo, shi)


def _mla_pre(proj, q_norm, kv_norm, wuq, rope_tabs):
    cc = COL_C // 512

    def tab_idx(i):
        return (jnp.where(i < N_PROMPT_TILES, TILES_PER_SAMPLE, (i - N_PROMPT_TILES) % TILES_PER_SAMPLE), 0)

    tab_spec = pl.BlockSpec((TILE, HEAD_G), tab_idx)
    return pl.pallas_call(
        _mla_pre_kernel,
        grid=(N_TILES,),
        in_specs=[pl.BlockSpec((TILE, 512), lambda i: (i, cc)),
                  pl.BlockSpec((1, C_Q_LORA), lambda i: (0, 0)),
                  pl.BlockSpec((1, C_KV_LORA), lambda i: (0, 0)),
                  pl.BlockSpec((C_Q_LORA, C_HEADS * HEAD_G), lambda i: (0, 0)),
                  tab_spec, tab_spec, tab_spec],
        out_specs=[pl.BlockSpec((TILE, C_HEADS * HEAD_G), lambda i: (i, 0)),
                   pl.BlockSpec((TILE, C_KV_LORA), lambda i: (i, 0)),
                   pl.BlockSpec((TILE, HEAD_G), lambda i: (i, 0))],
        out_shape=[jax.ShapeDtypeStruct((N_TOK, C_HEADS * HEAD_G), BF16),
                   jax.ShapeDtypeStruct((N_TOK, C_KV_LORA), F32),
                   jax.ShapeDtypeStruct((N_TOK, HEAD_G), F32)],
        compiler_params=_cparams(("parallel",)),
        name="mla_pre",
    )(proj, q_norm, kv_norm, wuq, *rope_tabs)


def _mla_kv_kernel(ckv_ref, kpe_ref, wuk_ref, wuv_ref, k_ref, v_ref):
    ckv = ckv_ref[...].astype(BF16)
    kn = _dot(ckv, wuk_ref[...])
    kpe = kpe_ref[...]
    for h in range(C_HEADS):
        sl = slice(h * HEAD_G, (h + 1) * HEAD_G)
        k_ref[:, sl] = (kn[:, sl] + kpe).astype(BF16)
    v_ref[...] = _dot(ckv, wuv_ref[...]).astype(BF16)


def _mla_kv(ckv_all, kpe_all, wuk, wuv):
    n = ckv_all.shape[0]
    return pl.pallas_call(
        _mla_kv_kernel,
        grid=(n // TILE,),
        in_specs=[pl.BlockSpec((TILE, C_KV_LORA), lambda i: (i, 0)),
                  pl.BlockSpec((TILE, HEAD_G), lambda i: (i, 0)),
                  pl.BlockSpec((C_KV_LORA, C_HEADS * HEAD_G), lambda i: (0, 0)),
                  pl.BlockSpec((C_KV_LORA, C_W), lambda i: (0, 0))],
        out_specs=[pl.BlockSpec((TILE, C_HEADS * HEAD_G), lambda i: (i, 0)),
                   pl.BlockSpec((TILE, C_W), lambda i: (i, 0))],
        out_shape=[jax.ShapeDtypeStruct((n, C_HEADS * HEAD_G), BF16),
                   jax.ShapeDtypeStruct((n, C_W), BF16)],
        compiler_params=_cparams(("parallel",)),
        name="mla_kv",
    )(ckv_all, kpe_all, wuk, wuv)


def _attn_kernel(q_ref, k_ref, v_ref, o_ref):
    for h in range(C_HEADS):
        sl = slice(h * HEAD_G, (h + 1) * HEAD_G)
        s = _dot(q_ref[:, sl], k_ref[0, :, sl], NT)
        e = jnp.exp(s - jnp.max(s, axis=1, keepdims=True))
        den = jnp.sum(e, axis=1, keepdims=True)
        o = _dot(e.astype(BF16), v_ref[0, :, h * C_V:(h + 1) * C_V])
        o_ref[:, h * C_V:(h + 1) * C_V] = o / den


def _attention(q, k, v, tq):
    n_seq, lk, _ = k.shape
    lq = q.shape[0] // n_seq
    nqb = lq // tq
    return pl.pallas_call(
        _attn_kernel,
        grid=(n_seq, nqb),
        in_specs=[pl.BlockSpec((tq, C_HEADS * HEAD_G), lambda s, j: (s * nqb + j, 0)),
                  pl.BlockSpec((1, lk, C_HEADS * HEAD_G), lambda s, j: (s, 0, 0)),
                  pl.BlockSpec((1, lk, C_W), lambda s, j: (s, 0, 0))],
        out_specs=pl.BlockSpec((tq, C_W), lambda s, j: (s * nqb + j, 0)),
        out_shape=jax.ShapeDtypeStruct((q.shape[0], C_W), F32),
        compiler_params=_cparams(("parallel", "parallel")),
        name="mla_attention",
    )(q, k, v)


def _merge_kernel(x_ref, rw0_ref, rw1_ref, ml0_ref, ml1_ref, yc_ref, ga_ref, gb_ref, gc_ref, og_ref,
                  g1_ref, sh2_ref, sc2_ref, rwlw_ref, rwlb_ref, mllw_ref, mllb_ref,
                  pa_ref, pb_ref, pc_ref, wo_ref, l1w_ref, l1b_ref, rtw_ref, rtb_ref,
                  x1_ref, u2_ref, lg_ref):
    ones_blk = _block_ones(A_W, A_HEAD)
    rw0 = rw0_ref[...]
    rw1 = rw1_ref[...]
    ya = _head_norm(rw0[:, 0:256] + rw1[:, 0:256], ones_blk, rwlw_ref[...], rwlb_ref[...], A_GN_EPS)
    ya = (ya + rw0[:, 256:512] + rw1[:, 256:512]) * rw0[:, 512:768]
    yb = _head_norm(ml0_ref[...] + ml1_ref[...], ones_blk, mllw_ref[...], mllb_ref[...], LN_EPS)
    yb = yb * _sigmoid(og_ref[...])
    merged = (_sigmoid(ga_ref[...]) * _dot(ya.astype(BF16), pa_ref[...])
              + _sigmoid(gb_ref[...]) * _dot(yb.astype(BF16), pb_ref[...])
              + _sigmoid(gc_ref[...]) * _dot(yc_ref[...].astype(BF16), pc_ref[...]))
    mix = _dot(merged.astype(BF16), wo_ref[...])
    x1 = _layer_norm(DN_ALPHA * x_ref[...] + g1_ref[0] * mix, l1w_ref[...], l1b_ref[...])
    x1_ref[...] = x1
    u2 = x1 * (1.0 + sc2_ref[0]) + sh2_ref[0]
    u2_ref[...] = u2.astype(BF16)
    lg_ref[...] = _mm(u2, rtw_ref[...], 3) + rtb_ref[...]


def _merge(x, rw, ml, yc, proj, modt, rwlw, rwlb, mllw, mllb, pa, pb, pc, wo, l1w, l1b, rtw, rtb):
    gcol = COL_G // 1024

    def row(shape):
        return pl.BlockSpec(shape, lambda i: (0, 0))

    def mod(kk):
        return pl.BlockSpec((1, 1, D_MODEL), lambda i: (i, 0, kk))

    return pl.pallas_call(
        _merge_kernel,
        grid=(N_TILES,),
        in_specs=[pl.BlockSpec((TILE, D_MODEL), lambda i: (i, 0)),
                  pl.BlockSpec((TILE, 768), lambda i: (i, 0)),
                  pl.BlockSpec((TILE, 768), lambda i: (i, 0)),
                  pl.BlockSpec((TILE, B_W), lambda i: (i, 0)),
                  pl.BlockSpec((TILE, B_W), lambda i: (i, 0)),
                  pl.BlockSpec((TILE, C_W), lambda i: (i, 0)),
                  pl.BlockSpec((TILE, D_MODEL), lambda i: (i, gcol)),
                  pl.BlockSpec((TILE, D_MODEL), lambda i: (i, gcol + 1)),
                  pl.BlockSpec((TILE, D_MODEL), lambda i: (i, gcol + 2)),
                  pl.BlockSpec((TILE, B_W), lambda i: (i, (COL_B + 768) // B_W)),
                  mod(2), mod(3), mod(4),
                  row((1, A_W)), row((1, A_W)), row((1, B_W)), row((1, B_W)),
                  row((A_W, D_MODEL)), row((B_W, D_MODEL)), row((C_W, D_MODEL)), row((D_MODEL, D_MODEL)),
                  row((1, D_MODEL)), row((1, D_MODEL)), row((D_MODEL, 128)), row((1, 128))],
        out_specs=[pl.BlockSpec((TILE, D_MODEL), lambda i: (i, 0)),
                   pl.BlockSpec((TILE, D_MODEL), lambda i: (i, 0)),
                   pl.BlockSpec((TILE, 128), lambda i: (i, 0))],
        out_shape=[jax.ShapeDtypeStruct((N_TOK, D_MODEL), F32),
                   jax.ShapeDtypeStruct((N_TOK, D_MODEL), BF16),
                   jax.ShapeDtypeStruct((N_TOK, 128), F32)],
        compiler_params=_cparams(("parallel",)),
        name="merge_postnorm_router",
    )(x, rw[0], rw[1], ml[0], ml[1], yc, proj, proj, proj, proj, modt, modt, modt,
      rwlw, rwlb, mllw, mllb, pa, pb, pc, wo, l1w, l1b, rtw, rtb)


N_ASSIGN = N_TOK * TOP_K
N_ROW_BLOCKS = N_ASSIGN // MOE_BLOCK
N_ITEMS = N_ROW_BLOCKS + N_EXPERTS


def _moe_kernel(blk_s, exp_s, lo_s, hi_s, init_s, x_ref, w1g_ref, w1l_ref, b1g_ref, b1l_ref, w2_ref, b2_ref,
                rw_ref, y_ref):
    w = pl.program_id(0)

    @pl.when(init_s[w] == 1)
    def _():
        y_ref[...] = jnp.zeros_like(y_ref)

    @pl.when(hi_s[w] > lo_s[w])
    def _():
        x = x_ref[...]
        hg = jnp.minimum(_dot(x, w1g_ref[0, 0]) + b1g_ref[0, 0], SWIGLU_LIMIT)
        hl = jnp.clip(_dot(x, w1l_ref[0, 0]) + b1l_ref[0, 0], -SWIGLU_LIMIT, SWIGLU_LIMIT)
        act = hg * _sigmoid(SWIGLU_ALPHA * hg) * (hl + 1.0)
        y = _dot(act.astype(BF16), w2_ref[0, 0].astype(BF16)) + b2_ref[0, 0]
        rid = lax.broadcasted_iota(jnp.int32, (MOE_BLOCK, 1), 0)
        mine = (rid >= lo_s[w]) & (rid < hi_s[w])
        y_ref[...] += jnp.where(mine, y * rw_ref[...], 0.0)


def _moe_experts(l, x_sorted, items, w1g, w1l, b1g, b1l, w2, b2, w_sorted):
    def wspec(shape):
        return pl.BlockSpec((1, 1) + shape, lambda i, blk, ex, *_: (l, ex[i], 0, 0))

    grid_spec = pltpu.PrefetchScalarGridSpec(
        num_scalar_prefetch=5,
        grid=(N_ITEMS,),
        in_specs=[pl.BlockSpec((MOE_BLOCK, D_MODEL), lambda i, blk, *_: (blk[i], 0)),
                  wspec((D_MODEL, D_EXPERT)), wspec((D_MODEL, D_EXPERT)),
                  wspec((1, D_EXPERT)), wspec((1, D_EXPERT)),
                  wspec((D_EXPERT, D_MODEL)), wspec((1, D_MODEL)),
                  pl.BlockSpec((MOE_BLOCK, 1), lambda i, blk, *_: (blk[i], 0))],
        out_specs=pl.BlockSpec((MOE_BLOCK, D_MODEL), lambda i, blk, *_: (blk[i], 0)),
    )
    return pl.pallas_call(
        _moe_kernel,
        grid_spec=grid_spec,
        out_shape=jax.ShapeDtypeStruct((N_ASSIGN, D_MODEL), F32),
        compiler_params=_cparams(("arbitrary",)),
        name="moe_experts",
    )(*items, x_sorted, w1g, w1l, b1g, b1l, w2, b2, w_sorted)


def _final_kernel(x_ref, f_ref, g2_ref, w_ref, b_ref, o_ref):
    ffn = (f_ref[0] + f_ref[1]) + (f_ref[2] + f_ref[3])
    o_ref[...] = _layer_norm(DN_ALPHA * x_ref[...] + g2_ref[0] * ffn, w_ref[...], b_ref[...])


def _final_norm(x1, y_slots, modt, w, b):
    return pl.pallas_call(
        _final_kernel,
        grid=(N_TILES,),
        in_specs=[pl.BlockSpec((TILE, D_MODEL), lambda i: (i, 0)),
                  pl.BlockSpec((TOP_K, TILE, D_MODEL), lambda i: (0, i, 0)),
                  pl.BlockSpec((1, 1, D_MODEL), lambda i: (i, 0, 5)),
                  pl.BlockSpec((1, D_MODEL), lambda i: (0, 0)),
                  pl.BlockSpec((1, D_MODEL), lambda i: (0, 0))],
        out_specs=pl.BlockSpec((TILE, D_MODEL), lambda i: (i, 0)),
        out_shape=jax.ShapeDtypeStruct((N_TOK, D_MODEL), F32),
        compiler_params=_cparams(("parallel",)),
        name="ffn_postnorm",
    )(x1, y_slots, modt, w, b)


def _route(logits):
    top_v, top_e = lax.top_k(logits, TOP_K)
    top_w = jax.nn.softmax(top_v, axis=-1)
    flat_e = top_e.reshape(-1).astype(jnp.int32)
    idx = jnp.arange(N_ASSIGN, dtype=jnp.int32)
    _, order, w_sorted = lax.sort((flat_e, idx, top_w.reshape(-1)), num_keys=1)
    _, inv = lax.sort((order, idx), num_keys=1)
    ex = jnp.arange(N_EXPERTS, dtype=jnp.int32)
    counts = jnp.sum((flat_e[:, None] == ex[None, :]).astype(jnp.int32), axis=0)
    end = jnp.cumsum(counts)
    start = end - counts
    first_blk = start // MOE_BLOCK
    n_items = jnp.where(counts > 0, (end - 1) // MOE_BLOCK - first_blk + 1, 0)
    item_end = jnp.cumsum(n_items)
    item_start = item_end - n_items
    w = jnp.arange(N_ITEMS, dtype=jnp.int32)
    valid = w < item_end[-1]
    e_w = jnp.minimum(jnp.sum((item_end[None, :] <= w[:, None]).astype(jnp.int32), axis=1), N_EXPERTS - 1)
    pick = (e_w[:, None] == ex[None, :]).astype(jnp.int32)
    look = lambda tab: jnp.sum(pick * tab[None, :], axis=1)
    blk = jnp.where(valid, look(first_blk) + w - look(item_start), N_ROW_BLOCKS - 1)
    lo = jnp.where(valid, jnp.maximum(look(start) - blk * MOE_BLOCK, 0), 0)
    hi = jnp.where(valid, jnp.minimum(look(end) - blk * MOE_BLOCK, MOE_BLOCK), 0)
    e_last = jnp.max(jnp.where(counts > 0, ex, 0))
    e_w = jnp.where(valid, e_w, e_last)
    init = jnp.concatenate([jnp.ones((1,), jnp.int32), (blk[1:] != blk[:-1]).astype(jnp.int32)])
    items = tuple(a.astype(jnp.int32) for a in (blk, e_w, lo, hi, init))
    return order // TOP_K, w_sorted, inv, items


def _rope_tables():
    rows = DEC_SEQ // GRID_W
    r, col = jnp.meshgrid(jnp.arange(rows, dtype=F32), jnp.arange(GRID_W, dtype=F32), indexing='ij')
    n_freq = C_ROPE // 4
    inv = 1.0 / (ROPE_BASE ** (jnp.arange(n_freq, dtype=F32) / n_freq))
    ang = jnp.concatenate([r.reshape(-1, 1) * inv, col.reshape(-1, 1) * inv], axis=-1)
    cos, sin = jnp.cos(ang), jnp.sin(ang)
    half = C_ROPE // 2
    one = jnp.ones((DEC_SEQ, C_NOPE), F32)
    zero = jnp.zeros((DEC_SEQ, C_NOPE), F32)
    tail1 = jnp.ones((DEC_SEQ, HEAD_G - C_NOPE - C_ROPE), F32)
    tail0 = jnp.zeros((DEC_SEQ, HEAD_G - C_NOPE - C_ROPE), F32)
    zh = jnp.zeros((DEC_SEQ, half), F32)
    t_cos = jnp.concatenate([one, cos, cos, tail1], axis=1)
    t_lo = jnp.concatenate([zero, -sin, zh, tail0], axis=1)
    t_hi = jnp.concatenate([zero, zh, sin, tail0], axis=1)
    ident = jnp.ones((TILE, HEAD_G), F32)
    nil = jnp.zeros((TILE, HEAD_G), F32)
    return (jnp.concatenate([t_cos, ident]), jnp.concatenate([t_lo, nil]), jnp.concatenate([t_hi, nil]))


def _pad_heads(w, n_heads, width):
    lead = w.shape[:-1]
    w = w.reshape(lead + (n_heads, width))
    w = jnp.pad(w, [(0, 0)] * len(lead) + [(0, 0), (0, HEAD_G - width)])
    return w.reshape(lead + (n_heads * HEAD_G,))


def kernel(x_prompt, x_sample, state_rwkv, state_mlstm_c, state_mlstm_n, state_mlstm_m, cache_mla_ckv,
           cache_mla_kpe, c, c_ctx, ada_w, ada_b, w_in, rw_mu, rw_w0, rw_w2, rw_a0, rw_a2, rw_g2, rw_kk,
           rw_ka, rw_rk, rw_ln_w, rw_ln_b, ml_conv, ml_gate_b, ml_ln_w, ml_ln_b, mla_q_norm, mla_wuq,
           mla_kv_norm, mla_wuk, mla_wuv, proj_a, proj_b, proj_c, w_out, ln1_w, ln1_b, router_w, router_b,
           moe_w1, moe_b1, moe_w2, moe_b2, ln2_w, ln2_b):
    L = DEPTH
    x = jnp.concatenate([x_prompt.reshape(-1, D_MODEL), x_sample.reshape(-1, D_MODEL)], axis=0)

    cond = jnp.concatenate([c_ctx[None], c, jnp.zeros((8 - 1 - DEC_BATCH, D_MODEL), F32)], axis=0)
    wa = w_in[:, :, 0:A_IN]
    wb = w_in[:, :, A_IN:A_IN + 1024]
    wbg = w_in[:, :, A_IN + 1024:A_IN + B_IN]
    wc = w_in[:, :, A_IN + B_IN:A_IN + B_IN + C_IN]
    wg = w_in[:, :, A_IN + B_IN + C_IN:]
    z = lambda n: jnp.zeros((L, D_MODEL, n), F32)
    w_proj = jnp.concatenate([
        wa, wg, wb,
        wc[:, :, 0:384], z(64), wc[:, :, 384:416], z(32),
        wbg[:, :, 0:8], z(120), wbg[:, :, 8:16], z(120)], axis=2).astype(BF16)
    gate_b = jnp.pad(ml_gate_b.reshape(L, 2, 1, 2 * B_HEADS), ((0, 0), (0, 0), (0, 0), (0, 128 - 2 * B_HEADS)))
    wuq = _pad_heads(mla_wuq, C_HEADS, C_NOPE + C_ROPE).astype(BF16)
    wuk = _pad_heads(mla_wuk, C_HEADS, C_NOPE).astype(BF16)
    wuv = mla_wuv.astype(BF16)
    pa, pb, pc, wo = proj_a.astype(BF16), proj_b.astype(BF16), proj_c.astype(BF16), w_out.astype(BF16)
    rtw = jnp.pad(router_w, ((0, 0), (0, 0), (0, 128 - N_EXPERTS)))
    rtb = jnp.pad(router_b, ((0, 0), (0, 128 - N_EXPERTS))).reshape(L, 1, 128)
    w1 = moe_w1.reshape(L, N_EXPERTS, D_MODEL, D_EXPERT, 2)
    w1g, w1l = w1[..., 0].astype(BF16), w1[..., 1].astype(BF16)
    b1 = moe_b1.reshape(L, N_EXPERTS, 1, D_EXPERT, 2)
    b1g, b1l = b1[..., 0], b1[..., 1]
    b2 = moe_b2.reshape(L, N_EXPERTS, 1, D_MODEL)
    rope_tabs = _rope_tables()
    row2 = lambda a: a.reshape(L, 1, -1)

    rw_s0 = jnp.concatenate([jnp.zeros((BATCH, L, 2, A_HEADS, A_HEAD, A_HEAD), F32),
                             jnp.swapaxes(state_rwkv, -1, -2)], axis=0)
    m_col = jnp.broadcast_to(state_mlstm_m[..., None, None], state_mlstm_m.shape + (B_DK, 1))
    ml_dec = jnp.concatenate([state_mlstm_c, state_mlstm_n[..., None], m_col,
                              jnp.zeros(state_mlstm_m.shape + (B_DK, 128 - B_DK - 2), F32)], axis=-1)
    ml_s0 = jnp.concatenate([jnp.zeros((BATCH,) + ml_dec.shape[1:], F32), ml_dec], axis=0)
    kpe_cache = jnp.pad(cache_mla_kpe, ((0, 0), (0, 0), (0, 0), (C_NOPE, HEAD_G - C_NOPE - C_ROPE)))

    mod = _modulation(cond, ada_w, ada_b)
    n_prompt = BATCH * SEQ
    outs = {k: [] for k in ('rw', 'mlc', 'ckv', 'kpe')}
    for l in range(L):
        modt = mod[l][_MOD_ROW_OF_TILE].reshape(N_TILES, 1, 6 * D_MODEL)
        proj = _in_projection(x, modt, w_proj[l])
        rw, rw_fin = _rwkv(proj, rw_s0[:, l], row2(rw_mu)[l], rw_w0[l][:, None], rw_w2[l], rw_a0[l][:, None],
                           rw_a2[l], rw_g2[l], row2(rw_kk)[l], row2(rw_ka)[l], row2(rw_rk)[l])
        ml, ml_fin = _mlstm(proj, ml_s0[:, l], ml_conv[l], gate_b[l])
        q, ckv, kpe = _mla_pre(proj, row2(mla_q_norm)[l], row2(mla_kv_norm)[l], wuq[l], rope_tabs)
        ckv_all = jnp.concatenate([ckv[:n_prompt]] + [
            t for s in range(DEC_BATCH)
            for t in (cache_mla_ckv[s, l], ckv[n_prompt + s * DEC_SEQ:n_prompt + (s + 1) * DEC_SEQ])], axis=0)
        kpe_all = jnp.concatenate([kpe[:n_prompt]] + [
            t for s in range(DEC_BATCH)
            for t in (kpe_cache[s, l], kpe[n_prompt + s * DEC_SEQ:n_prompt + (s + 1) * DEC_SEQ])], axis=0)
        kf, vf = _mla_kv(ckv_all, kpe_all, wuk[l], wuv[l])
        lk = PAST_LEN + DEC_SEQ
        yc_p = _attention(q[:n_prompt], kf[:n_prompt].reshape(BATCH, SEQ, -1),
                          vf[:n_prompt].reshape(BATCH, SEQ, -1), SEQ)
        yc_s = _attention(q[n_prompt:], kf[n_prompt:].reshape(DEC_BATCH, lk, -1),
                          vf[n_prompt:].reshape(DEC_BATCH, lk, -1), 256)
        yc = jnp.concatenate([yc_p, yc_s], axis=0)
        x1, u2, logits = _merge(x, rw, ml, yc, proj, modt, row2(rw_ln_w)[l], row2(rw_ln_b)[l],
                                row2(ml_ln_w)[l], row2(ml_ln_b)[l], pa[l], pb[l], pc[l], wo[l],
                                row2(ln1_w)[l], row2(ln1_b)[l], rtw[l], rtb[l])
        tok_sorted, w_sorted, inv, items = _route(logits[:, :N_EXPERTS])
        y_sorted = _moe_experts(l, u2[tok_sorted], items, w1g, w1l, b1g, b1l, moe_w2, b2, w_sorted[:, None])
        y_slots = y_sorted[inv.reshape(N_TOK, TOP_K).T.reshape(-1)].reshape(TOP_K, N_TOK, D_MODEL)
        x = _final_norm(x1, y_slots, modt, row2(ln2_w)[l], row2(ln2_b)[l])
        outs['rw'].append(jnp.swapaxes(jnp.stack([rw_fin[0][:BATCH], rw_fin[1][:BATCH]], axis=1), -1, -2))
        outs['mlc'].append(ml_fin[:BATCH])
        outs['ckv'].append(ckv[:n_prompt].reshape(BATCH, SEQ, C_KV_LORA))
        outs['kpe'].append(kpe[:n_prompt, C_NOPE:C_NOPE + C_ROPE].reshape(BATCH, SEQ, C_ROPE))

    stack = lambda k: jnp.stack(outs[k], axis=1)
    mlc = stack('mlc')
    return (x[:n_prompt].reshape(BATCH, SEQ, D_MODEL), x[n_prompt:].reshape(DEC_BATCH, DEC_SEQ, D_MODEL),
            stack('rw'), mlc[..., 0:B_DK], mlc[..., N_COL], mlc[..., 0, M_COL], stack('ckv'), stack('kpe'))
```

```python
import functools

import numpy as np
import jax
import jax.numpy as jnp
from jax import lax
from jax.experimental import pallas as pl
from jax.experimental.pallas import tpu as pltpu

F32 = jnp.float32
BF16 = jnp.bfloat16

D_MODEL = 1024
BATCH = 16
SEQ = 256
DEPTH = 4
DEC_BATCH = 2
DEC_SEQ = 4096
PAST_LEN = 256
GRID_W = 64

A_HEADS = 4
A_HEAD = 64
A_W = 256
A_GN_EPS = 64e-5
B_HEADS = 4
B_DK = 64
B_W = 256
C_HEADS = 8
C_NOPE = 64
C_ROPE = 32
C_V = 64
C_Q_LORA = 256
C_KV_LORA = 128
C_W = 512
ROPE_BASE = 10000.0
N_EXPERTS = 32
TOP_K = 4
D_EXPERT = 1024
SWIGLU_LIMIT = 7.0
SWIGLU_ALPHA = 1.702
MOE_BLOCK = 256
A_IN = 1024
B_IN = 1040
C_IN = 416
DN_ALPHA = (2 * DEPTH) ** 0.25
LN_EPS = 1e-5
RMS_EPS = 1e-6

TILE = 256
CHUNK = 64
N_TOK = BATCH * SEQ + DEC_BATCH * DEC_SEQ
N_TILES = N_TOK // TILE
N_PROMPT_TILES = BATCH * SEQ // TILE
TILES_PER_SAMPLE = DEC_SEQ // TILE
N_SEQ = BATCH + DEC_BATCH
HALO = 8

COL_A = 0
COL_G = 1024
COL_B = 4096
COL_C = 5120
COL_BG = 5632
N_PROJ = 5888

VMEM_LIMIT = 56 * 1024 * 1024


def _cparams(sem):
    return pltpu.CompilerParams(dimension_semantics=sem, vmem_limit_bytes=VMEM_LIMIT)


NN = ((1,), (0,))
NT = ((1,), (1,))
TN = ((0,), (0,))


def _dot(a, b, dims=NN):
    return lax.dot_general(a, b, (dims, ((), ())), preferred_element_type=F32)


def _split(x, n):
    parts, r = [], x
    for i in range(n):
        p = r.astype(BF16)
        parts.append(p)
        if i + 1 < n:
            r = r - p.astype(F32)
    return parts


def _mm(a, b, passes=1, dims=NN):
    if passes == 1:
        return _dot(a.astype(BF16), b.astype(BF16), dims)
    ah, al = _split(a, 2)
    bh, bl = _split(b, 2)
    return _dot(ah, bh, dims) + (_dot(al, bh, dims) + _dot(ah, bl, dims))


def _mm_exact_l(t01, x):
    x0, x1, x2 = _split(x, 3)
    return _dot(t01, x0) + (_dot(t01, x1) + _dot(t01, x2))


def _mm_exact_r(x, t01):
    x0, x1, x2 = _split(x, 3)
    return _dot(x0, t01) + (_dot(x1, t01) + _dot(x2, t01))


def _sigmoid(x):
    return 1.0 / (1.0 + jnp.exp(-x))


def _softplus(x):
    return jnp.maximum(x, 0.0) + jnp.log(1.0 + jnp.exp(-jnp.abs(x)))


def _block_ones(n, blk):
    r = lax.broadcasted_iota(jnp.int32, (n, n), 0) // blk
    c = lax.broadcasted_iota(jnp.int32, (n, n), 1) // blk
    return jnp.where(r == c, 1.0, 0.0).astype(BF16)


def _layer_norm(x, w, b):
    mu = jnp.mean(x, axis=-1, keepdims=True)
    xc = x - mu
    var = jnp.mean(xc * xc, axis=-1, keepdims=True)
    return xc * lax.rsqrt(var + LN_EPS) * w + b


def _head_norm(x, ones_blk, w, b, eps):
    mu = _mm_exact_r(x, ones_blk) * (1.0 / 64.0)
    xc = x - mu
    var = _mm_exact_r(xc * xc, ones_blk) * (1.0 / 64.0)
    return xc * lax.rsqrt(var + eps) * w + b


def _shifted(x, prev_row, next_row):
    rid = lax.broadcasted_iota(jnp.int32, x.shape, 0)
    xp = jnp.where(rid == 0, prev_row, pltpu.roll(x, 1, axis=0))
    xn = jnp.where(rid == x.shape[0] - 1, next_row, pltpu.roll(x, x.shape[0] - 1, axis=0))
    return xp, xn


def _schedule():
    seq_of_tile = np.concatenate([np.arange(BATCH), BATCH + np.repeat(np.arange(DEC_BATCH), TILES_PER_SAMPLE)])
    first_of_tile = np.ones(N_TILES, np.int32)
    last_of_tile = np.ones(N_TILES, np.int32)
    for s in range(DEC_BATCH):
        base = N_PROMPT_TILES + s * TILES_PER_SAMPLE
        first_of_tile[base + 1: base + TILES_PER_SAMPLE] = 0
        last_of_tile[base: base + TILES_PER_SAMPLE - 1] = 0
    tile_fwd = np.arange(N_TILES)
    tile_bwd = np.arange(N_TILES)
    for s in range(DEC_BATCH):
        base = N_PROMPT_TILES + s * TILES_PER_SAMPLE
        tile_bwd[base: base + TILES_PER_SAMPLE] = base + TILES_PER_SAMPLE - 1 - np.arange(TILES_PER_SAMPLE)
    tile = np.concatenate([tile_fwd, tile_bwd]).astype(np.int32)
    seq = seq_of_tile[tile].astype(np.int32)
    begins = np.concatenate([first_of_tile[tile_fwd], last_of_tile[tile_bwd]]).astype(np.int32)
    ends = np.concatenate([last_of_tile[tile_fwd], first_of_tile[tile_bwd]]).astype(np.int32)
    has_prev = (1 - first_of_tile).astype(np.int32)
    has_next = (1 - last_of_tile).astype(np.int32)
    return tile, seq, begins, ends, has_prev, has_next


_SCHED = _schedule()
_SEQ_OF_TILE = np.concatenate([np.arange(BATCH), BATCH + np.repeat(np.arange(DEC_BATCH), TILES_PER_SAMPLE)]).astype(np.int32)
_MOD_ROW_OF_TILE = np.concatenate([np.zeros(N_PROMPT_TILES), 1 + np.repeat(np.arange(DEC_BATCH), TILES_PER_SAMPLE)]).astype(np.int32)


def _direction_masks(d):
    row = lax.broadcasted_iota(jnp.int32, (CHUNK, CHUNK), 0)
    col = lax.broadcasted_iota(jnp.int32, (CHUNK, CHUNK), 1)
    diff = (row - col) * (1 - 2 * d)
    return diff >= 0, diff > 0, diff <= 0


def _mod_kernel(c_ref, w_ref, b_ref, o_ref):
    c = c_ref[...]
    o_ref[0] = _mm(c * _sigmoid(c), w_ref[0], 3) + b_ref[0]


def _modulation(cond, ada_w, ada_b):
    tn = 1536
    return pl.pallas_call(
        _mod_kernel,
        grid=(DEPTH, 6 * D_MODEL // tn),
        in_specs=[pl.BlockSpec((8, D_MODEL), lambda l, j: (0, 0)),
                  pl.BlockSpec((1, D_MODEL, tn), lambda l, j: (l, 0, j)),
                  pl.BlockSpec((1, 1, tn), lambda l, j: (l, 0, j))],
        out_specs=pl.BlockSpec((1, 8, tn), lambda l, j: (l, 0, j)),
        out_shape=jax.ShapeDtypeStruct((DEPTH, 8, 6 * D_MODEL), F32),
        compiler_params=_cparams(("parallel", "parallel")),
        name="ada_modulation",
    )(cond, ada_w, ada_b.reshape(DEPTH, 1, 6 * D_MODEL))


IN_TM = 512
IN_TN = N_PROJ // 2


def _inproj_kernel(x_ref, sh_ref, sc_ref, w_ref, o_ref):
    u = x_ref[...] * (1.0 + sc_ref[0]) + sh_ref[0]
    o_ref[...] = _dot(u.astype(BF16), w_ref[...])


def _in_projection(x, modt, w):
    rep = IN_TM // TILE
    return pl.pallas_call(
        _inproj_kernel,
        grid=(N_PROJ // IN_TN, N_TOK // IN_TM),
        in_specs=[pl.BlockSpec((IN_TM, D_MODEL), lambda j, i: (i, 0)),
                  pl.BlockSpec((1, 1, D_MODEL), lambda j, i: (rep * i, 0, 0)),
                  pl.BlockSpec((1, 1, D_MODEL), lambda j, i: (rep * i, 0, 1)),
                  pl.BlockSpec((D_MODEL, IN_TN), lambda j, i: (0, j))],
        out_specs=pl.BlockSpec((IN_TM, IN_TN), lambda j, i: (i, j)),
        out_shape=jax.ShapeDtypeStruct((N_TOK, N_PROJ), F32),
        compiler_params=_cparams(("parallel", "parallel")),
        name="in_projection",
    )(x, modt, modt, w)


RW_P = 1
RW_GRAM_P = 1
RW_UPD_P = 1
RW_INV_P = 1
RW_ST_P = 3
HB = A_HEADS * CHUNK


def _pre(x, passes):
    return tuple(_split(x, 2)) if passes == 3 else (x.astype(BF16),)


def _mmp(a, b, dims=NN):
    out = _dot(a[0], b[0], dims)
    if len(a) == 2 and len(b) == 2:
        out = out + (_dot(a[1], b[0], dims) + _dot(a[0], b[1], dims))
    return out


def _head_expand(m, same_head):
    return jnp.where(same_head, jnp.concatenate([m] * A_HEADS, axis=0), 0.0)


def _head_stack(m):
    return jnp.concatenate([m[:, h * A_HEAD:(h + 1) * A_HEAD] for h in range(A_HEADS)], axis=0)


def _head_unstack(m):
    return jnp.concatenate([m[h * CHUNK:(h + 1) * CHUNK, :] for h in range(A_HEADS)], axis=1)


def _rwkv_kernel(d, tile_s, seq_s, begin_s, end_s, hasprev_s, hasnext_s,
                 xa_ref, xp_ref, xn_ref, s0_ref, mu_ref, w0_ref, w2_ref, a0_ref, a2_ref, g2_ref,
                 kk_ref, ka_ref, rk_ref,
                 out_ref, sfin_ref,
                 st_scr):
    step = pl.program_id(0)
    tile = tile_s[step]

    @pl.when(begin_s[step] == 1)
    def _():
        st_scr[...] = s0_ref[0]

    x = xa_ref[...]
    prev_row = xp_ref[HALO - 1:HALO, :] * hasprev_s[tile].astype(F32)
    next_row = xn_ref[0:1, :] * hasnext_s[tile].astype(F32)
    xp, xn = _shifted(x, prev_row, next_row)
    xs = x + mu_ref[...] * (0.5 * (xp + xn) - x)
    r = xs[:, 0:256]
    k = xs[:, 256:512]
    v = xs[:, 512:768]
    wl = xs[:, 768:832]
    al = xs[:, 832:896]
    gl = xs[:, 896:1024]

    ones_blk = _block_ones(A_W, A_HEAD)
    kkv = k * kk_ref[...]
    nrm = jnp.sqrt(_mm_exact_r(kkv * kkv, ones_blk))
    kap = kkv / jnp.maximum(nrm, 1e-6)
    wpre = w0_ref[...] + _mm(jnp.tanh(wl), w2_ref[...], 3)
    lw = -jnp.exp(-_softplus(-wpre) - 0.5)
    a = _sigmoid(a0_ref[...] + _mm(al, a2_ref[...], 3))
    kd = k * (1.0 + (a - 1.0) * ka_ref[...])
    bonus = _mm_exact_r(r * kd * rk_ref[...], ones_blk) * v
    out_ref[:, 256:512] = bonus
    out_ref[:, 512:768] = _mm(_sigmoid(gl), g2_ref[...], 3)

    b = kap * a

    incl64, _, _ = _direction_masks(d)
    tinc = jnp.where(incl64, 1.0, 0.0).astype(BF16)
    row = lax.broadcasted_iota(jnp.int32, (HB, HB), 0)
    col = lax.broadcasted_iota(jnp.int32, (HB, HB), 1)
    same_head = (row // CHUNK) == (col // CHUNK)
    order = jnp.where(same_head, (row % CHUNK - col % CHUNK) * (1 - 2 * d), -1)
    incl = order >= 0
    strict = order > 0
    eye = jnp.where(row == col, 1.0, 0.0)
    xor = row ^ col
    level_masks = [(xor >> s) == 1 for s in range(CHUNK.bit_length() - 1)]
    n_chunks = TILE // CHUNK
    chunks = [c if d == 0 else n_chunks - 1 - c for c in range(n_chunks)]
    sl = [slice(c * CHUNK, (c + 1) * CHUNK) for c in chunks]
    rng = range(n_chunks)

    lwc = [lw[s] for s in sl]
    cs = [_mm_exact_l(tinc, x) for x in lwc]
    w_in = [jnp.exp(x) for x in cs]
    w_ex = [jnp.exp(cs[i] - lwc[i]) for i in rng]
    w_inv = [jnp.exp(-x) for x in cs]
    w_tot = [jnp.exp(jnp.sum(x, axis=0, keepdims=True)) for x in lwc]
    kt = [_head_expand(kap[sl[i]] * w_ex[i], same_head) for i in rng]
    rt = [_head_expand(r[sl[i]] * w_in[i], same_head) for i in rng]
    bt = [b[sl[i]] * w_inv[i] for i in rng]
    kdt = [kd[sl[i]] * w_inv[i] for i in rng]
    v_stack = [_head_stack(v[s]) for s in sl]
    v_st = [_pre(x, RW_P) for x in v_stack]
    v_su = v_st if RW_UPD_P == RW_P else [_pre(x, RW_UPD_P) for x in v_stack]
    gram = [_mm(jnp.concatenate([kt[i], rt[i]], axis=0),
                jnp.concatenate([bt[i]] * A_HEADS + [kdt[i]] * A_HEADS, axis=0), RW_GRAM_P, NT) for i in rng]
    l_b = [jnp.where(strict, g[:HB, :HB], 0.0) for g in gram]
    l_k = [jnp.where(strict, g[:HB, HB:], 0.0) for g in gram]
    m_b = [_pre(jnp.where(incl, g[HB:, :HB], 0.0), RW_P) for g in gram]
    m_k = [jnp.where(incl, g[HB:, HB:], 0.0) for g in gram]
    t_inv = [eye - jnp.where(level_masks[0], x, 0.0) for x in l_b]
    for mask in level_masks[1:]:
        tp = [_pre(t, RW_INV_P) for t in t_inv]
        w1 = [_mmp(tp[i], _pre(jnp.where(mask, l_b[i], 0.0), RW_INV_P)) for i in rng]
        t_inv = [t_inv[i] - _mmp(_pre(w1[i], RW_INV_P), tp[i]) for i in rng]
    lkv = [_mmp(_pre(l_k[i], RW_P), v_st[i]) for i in rng]
    xx = [_mm(t_inv[i], jnp.concatenate([kt[i], lkv[i]], axis=1), RW_P) for i in rng]
    xxp = [_pre(x, RW_P) for x in xx]
    mx = [_mmp(m_b[i], xxp[i]) for i in rng]
    rhat = [_pre(rt[i] - mx[i][:, :HB], RW_ST_P) for i in rng]
    y0 = [_mmp(_pre(m_k[i], RW_P), v_st[i]) - mx[i][:, HB:] for i in rng]
    xxu = xxp if RW_UPD_P == RW_P else [_pre(x, RW_UPD_P) for x in xx]
    bhp = [_pre(_head_expand(bt[i] * w_tot[i], same_head), RW_UPD_P) for i in rng]
    bx = [_mmp(bhp[i], xxu[i], TN) for i in rng]
    g_mat = [_pre(eye * w_tot[i] - bx[i][:, :HB], RW_ST_P) for i in rng]
    h_mat = [_mmp(_pre(_head_expand(kdt[i] * w_tot[i], same_head), RW_UPD_P), v_su[i], TN) - bx[i][:, HB:]
             for i in rng]

    st = st_scr[...]
    for i in rng:
        stp = _pre(st, RW_ST_P)
        out_ref[sl[i], 0:256] = _head_unstack(_mmp(rhat[i], stp) + y0[i])
        st = _mmp(g_mat[i], stp) + h_mat[i]
    st_scr[...] = st

    @pl.when(end_s[step] == 1)
    def _():
        sfin_ref[0] = st


def _rwkv_direction(d, proj, s0, mu, w0, w2, a0, a2, g2, kk, ka, rk):
    nrb = N_TOK // HALO
    per = TILE // HALO
    tile, seq, begins, ends, has_prev, has_next = _SCHED
    half = slice(d * N_TILES, (d + 1) * N_TILES)
    sched = tuple(jnp.asarray(a) for a in (tile[half], seq[half], begins[half], ends[half], has_prev, has_next))

    def const(shape):
        return pl.BlockSpec(shape, lambda i, *_: (0,) * len(shape))

    grid_spec = pltpu.PrefetchScalarGridSpec(
        num_scalar_prefetch=6,
        grid=(N_TILES,),
        in_specs=[
            pl.BlockSpec((TILE, A_IN), lambda i, t, *_: (t[i], COL_A // A_IN)),
            pl.BlockSpec((HALO, A_IN), lambda i, t, *_: (jnp.maximum(t[i] * per - 1, 0), 0)),
            pl.BlockSpec((HALO, A_IN), lambda i, t, *_: (jnp.minimum(t[i] * per + per, nrb - 1), 0)),
            pl.BlockSpec((1, HB, A_HEAD), lambda i, t, s, *_: (s[i], 0, 0)),
            const((1, A_IN)),
            const((1, A_W)), const((64, A_W)), const((1, A_W)), const((64, A_W)),
            const((128, A_W)), const((1, A_W)), const((1, A_W)), const((1, A_W)),
        ],
        out_specs=[
            pl.BlockSpec((TILE, 768), lambda i, t, *_: (t[i], 0)),
            pl.BlockSpec((1, HB, A_HEAD), lambda i, t, s, *_: (s[i], 0, 0)),
        ],
        scratch_shapes=[pltpu.VMEM((HB, A_HEAD), F32)],
    )
    return pl.pallas_call(
        functools.partial(_rwkv_kernel, d),
        grid_spec=grid_spec,
        out_shape=[jax.ShapeDtypeStruct((N_TOK, 768), F32),
                   jax.ShapeDtypeStruct((N_SEQ, HB, A_HEAD), F32)],
        compiler_params=_cparams(("arbitrary",)),
        name="rwkv7_mixer_fwd" if d == 0 else "rwkv7_mixer_bwd",
    )(*sched, proj, proj, proj, s0, mu, w0, w2, a0, a2, g2, kk, ka, rk)


def _rwkv(proj, s0, mu, w0, w2, a0, a2, g2, kk, ka, rk):
    outs = [_rwkv_direction(d, proj, s0[:, d].reshape(N_SEQ, HB, A_HEAD), mu, w0[d], w2[d], a0[d], a2[d],
                            g2, kk, ka, rk) for d in range(2)]
    fin = tuple(o[1].reshape(N_SEQ, A_HEADS, A_HEAD, A_HEAD) for o in outs)
    return (outs[0][0], outs[1][0]), fin


ML_P = 1
N_COL = B_DK
M_COL = B_DK + 1
NEG = -1e30


def _per_head_col(x, first):
    return jnp.concatenate([x[:, first + h:first + h + 1] for h in range(B_HEADS)], axis=0)


def _per_head_row(x_t, first):
    return jnp.concatenate([x_t[first + h:first + h + 1, :] for h in range(B_HEADS)], axis=1)


def _head_fill(x, first):
    return jnp.concatenate([jnp.broadcast_to(x[:, first + h:first + h + 1], (CHUNK, 1)) for h in range(B_HEADS)],
                           axis=0)


def _head_max(x):
    return jnp.concatenate(
        [jnp.broadcast_to(jnp.max(x[h * CHUNK:(h + 1) * CHUNK], axis=0, keepdims=True), (CHUNK, 1))
         for h in range(B_HEADS)], axis=0)


def _mlstm_kernel(d, tile_s, seq_s, begin_s, end_s, hasprev_s, hasnext_s,
                  xb_ref, xp_ref, xn_ref, gt_ref, s0_ref, cw_ref, gb_ref,
                  out_ref, sfin_ref,
                  cn_scr, m_scr):
    step = pl.program_id(0)
    tile = tile_s[step]

    @pl.when(begin_s[step] == 1)
    def _():
        s0 = s0_ref[0]
        cn_scr[...] = s0
        m_scr[...] = s0[:, M_COL:M_COL + 1]

    qk = xb_ref[:, 0:512]
    prev_row = xp_ref[HALO - 1:HALO, 0:512] * hasprev_s[tile].astype(F32)
    next_row = xn_ref[0:1, 0:512] * hasnext_s[tile].astype(F32)
    qp, qn = _shifted(qk, prev_row, next_row)
    conv = cw_ref[0:1, :] * qp + cw_ref[1:2, :] * qk + cw_ref[2:3, :] * qn
    act = conv * _sigmoid(conv)
    q = act[:, 0:256]
    k = act[:, 256:512] * (B_DK ** -0.5)
    v = xb_ref[:, 512:768]
    li = gt_ref[...] + gb_ref[...]
    lf = -_softplus(-li)

    incl64, _, incl64_t = _direction_masks(d)
    tinc = jnp.where(incl64, 1.0, 0.0).astype(BF16)
    tinc_t = jnp.where(incl64_t, 1.0, 0.0).astype(BF16)
    row = lax.broadcasted_iota(jnp.int32, (HB, HB), 0)
    col = lax.broadcasted_iota(jnp.int32, (HB, HB), 1)
    same_head = (row // CHUNK) == (col // CHUNK)
    causal = jnp.where(same_head, (row % CHUNK - col % CHUNK) * (1 - 2 * d), -1) >= 0
    lane = lax.broadcasted_iota(jnp.int32, (HB, 128), 1)
    n_chunks = TILE // CHUNK
    chunks = [c if d == 0 else n_chunks - 1 - c for c in range(n_chunks)]
    sl = [slice(c * CHUNK, (c + 1) * CHUNK) for c in chunks]
    rng = range(n_chunks)
    fg = B_HEADS

    li_c = [li[s] for s in sl]
    lf_c = [lf[s] for s in sl]
    b_c = [_mm_exact_l(tinc, x) for x in lf_c]
    b_r = [_mm_exact_r(x.T, tinc_t) for x in lf_c]
    li_r = [x.T for x in li_c]
    bc = [_per_head_col(x, fg) for x in b_c]
    lic = [_per_head_col(x, 0) for x in li_c]
    br = [_per_head_row(x, fg) for x in b_r]
    lir = [_per_head_row(x, 0) for x in li_r]
    be = [_head_fill(jnp.sum(x, axis=0, keepdims=True), fg) for x in lf_c]
    wlog = [be[i] - bc[i] + lic[i] for i in rng]
    wmax = [_head_max(x) for x in wlog]
    m_old, m_new = [], []
    m = m_scr[...]
    for i in rng:
        m_old.append(m)
        m = jnp.maximum(be[i] + m, wmax[i])
        m_new.append(m)
    m_scr[...] = m

    dlog = [jnp.where(causal, bc[i] - br[i] + lir[i], NEG) for i in rng]
    inter = [bc[i] + m_old[i] for i in rng]
    mj = [jnp.maximum(jnp.max(dlog[i], axis=1, keepdims=True), inter[i]) for i in rng]
    q_exp = [_pre(_head_expand(q[s], same_head), ML_P) for s in sl]
    k_exp = [_head_expand(k[s], same_head) for s in sl]
    s_mat = [_mmp(q_exp[i], _pre(jnp.concatenate([k[sl[i]]] * B_HEADS, axis=0), ML_P), NT)
             * jnp.exp(dlog[i] - mj[i]) for i in rng]
    e_int = [jnp.exp(inter[i] - mj[i]) for i in rng]
    v_aug = [jnp.where(lane == N_COL, 1.0,
                       jnp.concatenate([_head_stack(v[s]), jnp.zeros((HB, 128 - B_DK), F32)], axis=1)) for s in sl]
    sv = [_mm(s_mat[i], v_aug[i], ML_P) for i in rng]
    wk = [jnp.exp(wlog[i] - m_new[i]) for i in rng]
    dec = [jnp.exp(be[i] + m_old[i] - m_new[i]) for i in rng]
    kv = [_mm(k_exp[i], wk[i] * v_aug[i], ML_P, TN) for i in rng]
    floor = [jnp.exp(-x) for x in mj]

    cn = cn_scr[...]
    for i in rng:
        nd = sv[i] + e_int[i] * _mmp(q_exp[i], _pre(cn, ML_P))
        den = jnp.maximum(jnp.abs(nd[:, N_COL:N_COL + 1]), floor[i])
        out_ref[sl[i], :] = _head_unstack(nd[:, 0:B_DK] / den)
        cn = dec[i] * cn + kv[i]
    cn_scr[...] = cn

    @pl.when(end_s[step] == 1)
    def _():
        sfin_ref[0] = jnp.where(lane == M_COL, m, cn)


def _mlstm_direction(d, proj, s0, conv_w, gate_b):
    nrb = N_TOK // HALO
    per = TILE // HALO
    tile, seq, begins, ends, has_prev, has_next = _SCHED
    half = slice(d * N_TILES, (d + 1) * N_TILES)
    sched = tuple(jnp.asarray(a) for a in (tile[half], seq[half], begins[half], ends[half], has_prev, has_next))
    cb = COL_B // 1024
    grid_spec = pltpu.PrefetchScalarGridSpec(
        num_scalar_prefetch=6,
        grid=(N_TILES,),
        in_specs=[
            pl.BlockSpec((TILE, 1024), lambda i, t, *_: (t[i], cb)),
            pl.BlockSpec((HALO, 1024), lambda i, t, *_: (jnp.maximum(t[i] * per - 1, 0), cb)),
            pl.BlockSpec((HALO, 1024), lambda i, t, *_: (jnp.minimum(t[i] * per + per, nrb - 1), cb)),
            pl.BlockSpec((TILE, 128), lambda i, t, *_: (t[i], COL_BG // 128 + d)),
            pl.BlockSpec((1, HB, 128), lambda i, t, s, *_: (s[i], 0, 0)),
            pl.BlockSpec((3, 512), lambda i, *_: (0, 0)),
            pl.BlockSpec((1, 128), lambda i, *_: (0, 0)),
        ],
        out_specs=[
            pl.BlockSpec((TILE, B_W), lambda i, t, *_: (t[i], 0)),
            pl.BlockSpec((1, HB, 128), lambda i, t, s, *_: (s[i], 0, 0)),
        ],
        scratch_shapes=[pltpu.VMEM((HB, 128), F32), pltpu.VMEM((HB, 1), F32)],
    )
    return pl.pallas_call(
        functools.partial(_mlstm_kernel, d),
        grid_spec=grid_spec,
        out_shape=[jax.ShapeDtypeStruct((N_TOK, B_W), F32),
                   jax.ShapeDtypeStruct((N_SEQ, HB, 128), F32)],
        compiler_params=_cparams(("arbitrary",)),
        name="mlstm_mixer_fwd" if d == 0 else "mlstm_mixer_bwd",
    )(*sched, proj, proj, proj, proj, s0, conv_w, gate_b)


def _mlstm(proj, s0, conv_w, gate_b):
    outs = [_mlstm_direction(d, proj, s0[:, d].reshape(N_SEQ, HB, 128), conv_w, gate_b[d]) for d in range(2)]
    fin = jnp.stack([o[1].reshape(N_SEQ, B_HEADS, B_DK, 128) for o in outs], axis=1)
    return (outs[0][0], outs[1][0]), fin


HEAD_G = 128
ATT_SCALE = (C_NOPE + C_ROPE) ** -0.5


def _rope(x, cos, sin_lo, sin_hi):
    return x * cos + pltpu.roll(x, 16, axis=1) * sin_hi + pltpu.roll(x, HEAD_G - 16, axis=1) * sin_lo


def _mla_pre_kernel(xc_ref, qn_ref, kvn_ref, wuq_ref, cos_ref, slo_ref, shi_ref, q_ref, ckv_ref, kpe_ref):
    q_dn = xc_ref[:, 0:C_Q_LORA]
    qn = q_dn * lax.rsqrt(jnp.mean(q_dn * q_dn, axis=-1, keepdims=True) + RMS_EPS) * qn_ref[...]
    q = _dot(qn.astype(BF16), wuq_ref[...])
    cos, slo, shi = cos_ref[...], slo_ref[...], shi_ref[...]
    for h in range(C_HEADS):
        sl = slice(h * HEAD_G, (h + 1) * HEAD_G)
        q_ref[:, sl] = (_rope(q[:, sl], cos, slo, shi) * ATT_SCALE).astype(BF16)
    kv_dn = xc_ref[:, C_Q_LORA:C_Q_LORA + C_KV_LORA]
    ckv_ref[...] = kv_dn * lax.rsqrt(jnp.mean(kv_dn * kv_dn, axis=-1, keepdims=True) + RMS_EPS) * kvn_ref[...]
    kpe_ref[...] = _rope(xc_ref[:, 384:512], cos, slo, shi)


def _mla_pre(proj, q_norm, kv_norm, wuq, rope_tabs):
    cc = COL_C // 512

    def tab_idx(i):
        return (jnp.where(i < N_PROMPT_TILES, TILES_PER_SAMPLE, (i - N_PROMPT_TILES) % TILES_PER_SAMPLE), 0)

    tab_spec = pl.BlockSpec((TILE, HEAD_G), tab_idx)
    return pl.pallas_call(
        _mla_pre_kernel,
        grid=(N_TILES,),
        in_specs=[pl.BlockSpec((TILE, 512), lambda i: (i, cc)),
                  pl.BlockSpec((1, C_Q_LORA), lambda i: (0, 0)),
                  pl.BlockSpec((1, C_KV_LORA), lambda i: (0, 0)),
                  pl.BlockSpec((C_Q_LORA, C_HEADS * HEAD_G), lambda i: (0, 0)),
                  tab_spec, tab_spec, tab_spec],
        out_specs=[pl.BlockSpec((TILE, C_HEADS * HEAD_G), lambda i: (i, 0)),
                   pl.BlockSpec((TILE, C_KV_LORA), lambda i: (i, 0)),
                   pl.BlockSpec((TILE, HEAD_G), lambda i: (i, 0))],
        out_shape=[jax.ShapeDtypeStruct((N_TOK, C_HEADS * HEAD_G), BF16),
                   jax.ShapeDtypeStruct((N_TOK, C_KV_LORA), F32),
                   jax.ShapeDtypeStruct((N_TOK, HEAD_G), F32)],
        compiler_params=_cparams(("parallel",)),
        name="mla_pre",
    )(proj, q_norm, kv_norm, wuq, *rope_tabs)


def _mla_kv_kernel(ckv_ref, kpe_ref, wuk_ref, wuv_ref, k_ref, v_ref):
    ckv = ckv_ref[...].astype(BF16)
    kn = _dot(ckv, wuk_ref[...])
    kpe = kpe_ref[...]
    for h in range(C_HEADS):
        sl = slice(h * HEAD_G, (h + 1) * HEAD_G)
        k_ref[:, sl] = (kn[:, sl] + kpe).astype(BF16)
    v_ref[...] = _dot(ckv, wuv_ref[...]).astype(BF16)


def _mla_kv(ckv_all, kpe_all, wuk, wuv):
    n = ckv_all.shape[0]
    return pl.pallas_call(
        _mla_kv_kernel,
        grid=(n // TILE,),
        in_specs=[pl.BlockSpec((TILE, C_KV_LORA), lambda i: (i, 0)),
                  pl.BlockSpec((TILE, HEAD_G), lambda i: (i, 0)),
                  pl.BlockSpec((C_KV_LORA, C_HEADS * HEAD_G), lambda i: (0, 0)),
                  pl.BlockSpec((C_KV_LORA, C_W), lambda i: (0, 0))],
        out_specs=[pl.BlockSpec((TILE, C_HEADS * HEAD_G), lambda i: (i, 0)),
                   pl.BlockSpec((TILE, C_W), lambda i: (i, 0))],
        out_shape=[jax.ShapeDtypeStruct((n, C_HEADS * HEAD_G), BF16),
                   jax.ShapeDtypeStruct((n, C_W), BF16)],
        compiler_params=_cparams(("parallel",)),
        name="mla_kv",
    )(ckv_all, kpe_all, wuk, wuv)


def _attn_kernel(q_ref, k_ref, v_ref, o_ref):
    for h in range(C_HEADS):
        sl = slice(h * HEAD_G, (h + 1) * HEAD_G)
        s = _dot(q_ref[:, sl], k_ref[0, :, sl], NT)
        e = jnp.exp(s - jnp.max(s, axis=1, keepdims=True))
        den = jnp.sum(e, axis=1, keepdims=True)
        o = _dot(e.astype(BF16), v_ref[0, :, h * C_V:(h + 1) * C_V])
        o_ref[:, h * C_V:(h + 1) * C_V] = o / den


def _attention(q, k, v, tq):
    n_seq, lk, _ = k.shape
    lq = q.shape[0] // n_seq
    nqb = lq // tq
    return pl.pallas_call(
        _attn_kernel,
        grid=(n_seq, nqb),
        in_specs=[pl.BlockSpec((tq, C_HEADS * HEAD_G), lambda s, j: (s * nqb + j, 0)),
                  pl.BlockSpec((1, lk, C_HEADS * HEAD_G), lambda s, j: (s, 0, 0)),
                  pl.BlockSpec((1, lk, C_W), lambda s, j: (s, 0, 0))],
        out_specs=pl.BlockSpec((tq, C_W), lambda s, j: (s * nqb + j, 0)),
        out_shape=jax.ShapeDtypeStruct((q.shape[0], C_W), F32),
        compiler_params=_cparams(("parallel", "parallel")),
        name="mla_attention",
    )(q, k, v)


def _merge_kernel(x_ref, rw0_ref, rw1_ref, ml0_ref, ml1_ref, yc_ref, ga_ref, gb_ref, gc_ref, og_ref,
                  g1_ref, sh2_ref, sc2_ref, rwlw_ref, rwlb_ref, mllw_ref, mllb_ref,
                  pa_ref, pb_ref, pc_ref, wo_ref, l1w_ref, l1b_ref, rtw_ref, rtb_ref,
                  x1_ref, u2_ref, lg_ref):
    ones_blk = _block_ones(A_W, A_HEAD)
    rw0 = rw0_ref[...]
    rw1 = rw1_ref[...]
    ya = _head_norm(rw0[:, 0:256] + rw1[:, 0:256], ones_blk, rwlw_ref[...], rwlb_ref[...], A_GN_EPS)
    ya = (ya + rw0[:, 256:512] + rw1[:, 256:512]) * rw0[:, 512:768]
    yb = _head_norm(ml0_ref[...] + ml1_ref[...], ones_blk, mllw_ref[...], mllb_ref[...], LN_EPS)
    yb = yb * _sigmoid(og_ref[...])
    merged = (_sigmoid(ga_ref[...]) * _dot(ya.astype(BF16), pa_ref[...])
              + _sigmoid(gb_ref[...]) * _dot(yb.astype(BF16), pb_ref[...])
              + _sigmoid(gc_ref[...]) * _dot(yc_ref[...].astype(BF16), pc_ref[...]))
    mix = _dot(merged.astype(BF16), wo_ref[...])
    x1 = _layer_norm(DN_ALPHA * x_ref[...] + g1_ref[0] * mix, l1w_ref[...], l1b_ref[...])
    x1_ref[...] = x1
    u2 = x1 * (1.0 + sc2_ref[0]) + sh2_ref[0]
    u2_ref[...] = u2.astype(BF16)
    lg_ref[...] = _mm(u2, rtw_ref[...], 3) + rtb_ref[...]


def _merge(x, rw, ml, yc, proj, modt, rwlw, rwlb, mllw, mllb, pa, pb, pc, wo, l1w, l1b, rtw, rtb):
    gcol = COL_G // 1024

    def row(shape):
        return pl.BlockSpec(shape, lambda i: (0, 0))

    def mod(kk):
        return pl.BlockSpec((1, 1, D_MODEL), lambda i: (i, 0, kk))

    return pl.pallas_call(
        _merge_kernel,
        grid=(N_TILES,),
        in_specs=[pl.BlockSpec((TILE, D_MODEL), lambda i: (i, 0)),
                  pl.BlockSpec((TILE, 768), lambda i: (i, 0)),
                  pl.BlockSpec((TILE, 768), lambda i: (i, 0)),
                  pl.BlockSpec((TILE, B_W), lambda i: (i, 0)),
                  pl.BlockSpec((TILE, B_W), lambda i: (i, 0)),
                  pl.BlockSpec((TILE, C_W), lambda i: (i, 0)),
                  pl.BlockSpec((TILE, D_MODEL), lambda i: (i, gcol)),
                  pl.BlockSpec((TILE, D_MODEL), lambda i: (i, gcol + 1)),
                  pl.BlockSpec((TILE, D_MODEL), lambda i: (i, gcol + 2)),
                  pl.BlockSpec((TILE, B_W), lambda i: (i, (COL_B + 768) // B_W)),
                  mod(2), mod(3), mod(4),
                  row((1, A_W)), row((1, A_W)), row((1, B_W)), row((1, B_W)),
                  row((A_W, D_MODEL)), row((B_W, D_MODEL)), row((C_W, D_MODEL)), row((D_MODEL, D_MODEL)),
                  row((1, D_MODEL)), row((1, D_MODEL)), row((D_MODEL, 128)), row((1, 128))],
        out_specs=[pl.BlockSpec((TILE, D_MODEL), lambda i: (i, 0)),
                   pl.BlockSpec((TILE, D_MODEL), lambda i: (i, 0)),
                   pl.BlockSpec((TILE, 128), lambda i: (i, 0))],
        out_shape=[jax.ShapeDtypeStruct((N_TOK, D_MODEL), F32),
                   jax.ShapeDtypeStruct((N_TOK, D_MODEL), BF16),
                   jax.ShapeDtypeStruct((N_TOK, 128), F32)],
        compiler_params=_cparams(("parallel",)),
        name="merge_postnorm_router",
    )(x, rw[0], rw[1], ml[0], ml[1], yc, proj, proj, proj, proj, modt, modt, modt,
      rwlw, rwlb, mllw, mllb, pa, pb, pc, wo, l1w, l1b, rtw, rtb)


N_ASSIGN = N_TOK * TOP_K
N_ROW_BLOCKS = N_ASSIGN // MOE_BLOCK
N_ITEMS = N_ROW_BLOCKS + N_EXPERTS


def _moe_kernel(blk_s, exp_s, lo_s, hi_s, init_s, x_ref, w1g_ref, w1l_ref, b1g_ref, b1l_ref, w2_ref, b2_ref,
                rw_ref, y_ref):
    w = pl.program_id(0)

    @pl.when(init_s[w] == 1)
    def _():
        y_ref[...] = jnp.zeros_like(y_ref)

    @pl.when(hi_s[w] > lo_s[w])
    def _():
        x = x_ref[...]
        hg = jnp.minimum(_dot(x, w1g_ref[0, 0, 0]) + b1g_ref[0, 0], SWIGLU_LIMIT)
        hl = jnp.clip(_dot(x, w1l_ref[0, 0, 0]) + b1l_ref[0, 0], -SWIGLU_LIMIT, SWIGLU_LIMIT)
        act = hg * _sigmoid(SWIGLU_ALPHA * hg) * (hl + 1.0)
        y = _dot(act.astype(BF16), w2_ref[0, 0].astype(BF16)) + b2_ref[0, 0]
        rid = lax.broadcasted_iota(jnp.int32, (MOE_BLOCK, 1), 0)
        mine = (rid >= lo_s[w]) & (rid < hi_s[w])
        y_ref[...] = (y_ref[...].astype(F32) + jnp.where(mine, y * rw_ref[...], 0.0)).astype(y_ref.dtype)


def _moe_experts(l, x_sorted, items, w1s, b1g, b1l, w2, b2, w_sorted):
    def wspec(shape):
        return pl.BlockSpec((1, 1) + shape, lambda i, blk, ex, *_: (l, ex[i], 0, 0))

    def w1spec(half):
        return pl.BlockSpec((1, 1, 1, D_MODEL, D_EXPERT), lambda i, blk, ex, *_: (l, ex[i], half, 0, 0))

    grid_spec = pltpu.PrefetchScalarGridSpec(
        num_scalar_prefetch=5,
        grid=(N_ITEMS,),
        in_specs=[pl.BlockSpec((MOE_BLOCK, D_MODEL), lambda i, blk, *_: (blk[i], 0)),
                  w1spec(0), w1spec(1),
                  wspec((1, D_EXPERT)), wspec((1, D_EXPERT)),
                  wspec((D_EXPERT, D_MODEL)), wspec((1, D_MODEL)),
                  pl.BlockSpec((MOE_BLOCK, 1), lambda i, blk, *_: (blk[i], 0))],
        out_specs=pl.BlockSpec((MOE_BLOCK, D_MODEL), lambda i, blk, *_: (blk[i], 0)),
    )
    return pl.pallas_call(
        _moe_kernel,
        grid_spec=grid_spec,
        out_shape=jax.ShapeDtypeStruct((N_ASSIGN, D_MODEL), BF16),
        compiler_params=_cparams(("arbitrary",)),
        name="moe_experts",
    )(*items, x_sorted, w1s, w1s, b1g, b1l, w2, b2, w_sorted)


def _final_kernel(x_ref, f_ref, g2_ref, w_ref, b_ref, o_ref):
    ffn = (f_ref[0].astype(F32) + f_ref[1].astype(F32)) + (f_ref[2].astype(F32) + f_ref[3].astype(F32))
    o_ref[...] = _layer_norm(DN_ALPHA * x_ref[...] + g2_ref[0] * ffn, w_ref[...], b_ref[...])


def _final_norm(x1, y_slots, modt, w, b):
    return pl.pallas_call(
        _final_kernel,
        grid=(N_TILES,),
        in_specs=[pl.BlockSpec((TILE, D_MODEL), lambda i: (i, 0)),
                  pl.BlockSpec((TOP_K, TILE, D_MODEL), lambda i: (0, i, 0)),
                  pl.BlockSpec((1, 1, D_MODEL), lambda i: (i, 0, 5)),
                  pl.BlockSpec((1, D_MODEL), lambda i: (0, 0)),
                  pl.BlockSpec((1, D_MODEL), lambda i: (0, 0))],
        out_specs=pl.BlockSpec((TILE, D_MODEL), lambda i: (i, 0)),
        out_shape=jax.ShapeDtypeStruct((N_TOK, D_MODEL), F32),
        compiler_params=_cparams(("parallel",)),
        name="ffn_postnorm",
    )(x1, y_slots, modt, w, b)


def _route(logits):
    top_v, top_e = lax.top_k(logits, TOP_K)
    top_w = jax.nn.softmax(top_v, axis=-1)
    flat_e = top_e.reshape(-1).astype(jnp.int32)
    idx = jnp.arange(N_ASSIGN, dtype=jnp.int32)
    _, order, w_sorted = lax.sort((flat_e, idx, top_w.reshape(-1)), num_keys=1)
    _, inv = lax.sort((order, idx), num_keys=1)
    ex = jnp.arange(N_EXPERTS, dtype=jnp.int32)
    counts = jnp.sum((flat_e[:, None] == ex[None, :]).astype(jnp.int32), axis=0)
    end = jnp.cumsum(counts)
    start = end - counts
    first_blk = start // MOE_BLOCK
    n_items = jnp.where(counts > 0, (end - 1) // MOE_BLOCK - first_blk + 1, 0)
    item_end = jnp.cumsum(n_items)
    item_start = item_end - n_items
    w = jnp.arange(N_ITEMS, dtype=jnp.int32)
    valid = w < item_end[-1]
    e_w = jnp.minimum(jnp.sum((item_end[None, :] <= w[:, None]).astype(jnp.int32), axis=1), N_EXPERTS - 1)
    pick = (e_w[:, None] == ex[None, :]).astype(jnp.int32)
    look = lambda tab: jnp.sum(pick * tab[None, :], axis=1)
    blk = jnp.where(valid, look(first_blk) + w - look(item_start), N_ROW_BLOCKS - 1)
    lo = jnp.where(valid, jnp.maximum(look(start) - blk * MOE_BLOCK, 0), 0)
    hi = jnp.where(valid, jnp.minimum(look(end) - blk * MOE_BLOCK, MOE_BLOCK), 0)
    e_last = jnp.max(jnp.where(counts > 0, ex, 0))
    e_w = jnp.where(valid, e_w, e_last)
    init = jnp.concatenate([jnp.ones((1,), jnp.int32), (blk[1:] != blk[:-1]).astype(jnp.int32)])
    items = tuple(a.astype(jnp.int32) for a in (blk, e_w, lo, hi, init))
    return order // TOP_K, w_sorted, inv, items


def _rope_tables():
    rows = DEC_SEQ // GRID_W
    r, col = jnp.meshgrid(jnp.arange(rows, dtype=F32), jnp.arange(GRID_W, dtype=F32), indexing='ij')
    n_freq = C_ROPE // 4
    inv = 1.0 / (ROPE_BASE ** (jnp.arange(n_freq, dtype=F32) / n_freq))
    ang = jnp.concatenate([r.reshape(-1, 1) * inv, col.reshape(-1, 1) * inv], axis=-1)
    cos, sin = jnp.cos(ang), jnp.sin(ang)
    half = C_ROPE // 2
    one = jnp.ones((DEC_SEQ, C_NOPE), F32)
    zero = jnp.zeros((DEC_SEQ, C_NOPE), F32)
    tail1 = jnp.ones((DEC_SEQ, HEAD_G - C_NOPE - C_ROPE), F32)
    tail0 = jnp.zeros((DEC_SEQ, HEAD_G - C_NOPE - C_ROPE), F32)
    zh = jnp.zeros((DEC_SEQ, half), F32)
    t_cos = jnp.concatenate([one, cos, cos, tail1], axis=1)
    t_lo = jnp.concatenate([zero, -sin, zh, tail0], axis=1)
    t_hi = jnp.concatenate([zero, zh, sin, tail0], axis=1)
    ident = jnp.ones((TILE, HEAD_G), F32)
    nil = jnp.zeros((TILE, HEAD_G), F32)
    return (jnp.concatenate([t_cos, ident]), jnp.concatenate([t_lo, nil]), jnp.concatenate([t_hi, nil]))


def _pad_heads(w, n_heads, width):
    lead = w.shape[:-1]
    w = w.reshape(lead + (n_heads, width))
    w = jnp.pad(w, [(0, 0)] * len(lead) + [(0, 0), (0, HEAD_G - width)])
    return w.reshape(lead + (n_heads * HEAD_G,))


def kernel(x_prompt, x_sample, state_rwkv, state_mlstm_c, state_mlstm_n, state_mlstm_m, cache_mla_ckv,
           cache_mla_kpe, c, c_ctx, ada_w, ada_b, w_in, rw_mu, rw_w0, rw_w2, rw_a0, rw_a2, rw_g2, rw_kk,
           rw_ka, rw_rk, rw_ln_w, rw_ln_b, ml_conv, ml_gate_b, ml_ln_w, ml_ln_b, mla_q_norm, mla_wuq,
           mla_kv_norm, mla_wuk, mla_wuv, proj_a, proj_b, proj_c, w_out, ln1_w, ln1_b, router_w, router_b,
           moe_w1, moe_b1, moe_w2, moe_b2, ln2_w, ln2_b):
    L = DEPTH
    x = jnp.concatenate([x_prompt.reshape(-1, D_MODEL), x_sample.reshape(-1, D_MODEL)], axis=0)

    cond = jnp.concatenate([c_ctx[None], c, jnp.zeros((8 - 1 - DEC_BATCH, D_MODEL), F32)], axis=0)
    wa = w_in[:, :, 0:A_IN]
    wb = w_in[:, :, A_IN:A_IN + 1024]
    wbg = w_in[:, :, A_IN + 1024:A_IN + B_IN]
    wc = w_in[:, :, A_IN + B_IN:A_IN + B_IN + C_IN]
    wg = w_in[:, :, A_IN + B_IN + C_IN:]
    z = lambda n: jnp.zeros((L, D_MODEL, n), F32)
    w_proj = jnp.concatenate([
        wa, wg, wb,
        wc[:, :, 0:384], z(64), wc[:, :, 384:416], z(32),
        wbg[:, :, 0:8], z(120), wbg[:, :, 8:16], z(120)], axis=2).astype(BF16)
    gate_b = jnp.pad(ml_gate_b.reshape(L, 2, 1, 2 * B_HEADS), ((0, 0), (0, 0), (0, 0), (0, 128 - 2 * B_HEADS)))
    wuq = _pad_heads(mla_wuq, C_HEADS, C_NOPE + C_ROPE).astype(BF16)
    wuk = _pad_heads(mla_wuk, C_HEADS, C_NOPE).astype(BF16)
    wuv = mla_wuv.astype(BF16)
    pa, pb, pc, wo = proj_a.astype(BF16), proj_b.astype(BF16), proj_c.astype(BF16), w_out.astype(BF16)
    rtw = jnp.pad(router_w, ((0, 0), (0, 0), (0, 128 - N_EXPERTS)))
    rtb = jnp.pad(router_b, ((0, 0), (0, 128 - N_EXPERTS))).reshape(L, 1, 128)
    w1 = moe_w1.reshape(L, N_EXPERTS, D_MODEL, D_EXPERT, 2)
    w1s = jnp.moveaxis(w1, -1, 2).astype(BF16)
    b1 = moe_b1.reshape(L, N_EXPERTS, 1, D_EXPERT, 2)
    b1g, b1l = b1[..., 0], b1[..., 1]
    b2 = moe_b2.reshape(L, N_EXPERTS, 1, D_MODEL)
    rope_tabs = _rope_tables()
    row2 = lambda a: a.reshape(L, 1, -1)

    rw_s0 = jnp.concatenate([jnp.zeros((BATCH, L, 2, A_HEADS, A_HEAD, A_HEAD), F32),
                             jnp.swapaxes(state_rwkv, -1, -2)], axis=0)
    m_col = jnp.broadcast_to(state_mlstm_m[..., None, None], state_mlstm_m.shape + (B_DK, 1))
    ml_dec = jnp.concatenate([state_mlstm_c, state_mlstm_n[..., None], m_col,
                              jnp.zeros(state_mlstm_m.shape + (B_DK, 128 - B_DK - 2), F32)], axis=-1)
    ml_s0 = jnp.concatenate([jnp.zeros((BATCH,) + ml_dec.shape[1:], F32), ml_dec], axis=0)
    kpe_cache = jnp.pad(cache_mla_kpe, ((0, 0), (0, 0), (0, 0), (C_NOPE, HEAD_G - C_NOPE - C_ROPE)))

    mod = _modulation(cond, ada_w, ada_b)
    n_prompt = BATCH * SEQ
    outs = {k: [] for k in ('rw', 'mlc', 'ckv', 'kpe')}
    for l in range(L):
        modt = mod[l][_MOD_ROW_OF_TILE].reshape(N_TILES, 1, 6 * D_MODEL)
        proj = _in_projection(x, modt, w_proj[l])
        rw, rw_fin = _rwkv(proj, rw_s0[:, l], row2(rw_mu)[l], rw_w0[l][:, None], rw_w2[l], rw_a0[l][:, None],
                           rw_a2[l], rw_g2[l], row2(rw_kk)[l], row2(rw_ka)[l], row2(rw_rk)[l])
        ml, ml_fin = _mlstm(proj, ml_s0[:, l], ml_conv[l], gate_b[l])
        q, ckv, kpe = _mla_pre(proj, row2(mla_q_norm)[l], row2(mla_kv_norm)[l], wuq[l], rope_tabs)
        ckv_all = jnp.concatenate([ckv[:n_prompt]] + [
            t for s in range(DEC_BATCH)
            for t in (cache_mla_ckv[s, l], ckv[n_prompt + s * DEC_SEQ:n_prompt + (s + 1) * DEC_SEQ])], axis=0)
        kpe_all = jnp.concatenate([kpe[:n_prompt]] + [
            t for s in range(DEC_BATCH)
            for t in (kpe_cache[s, l], kpe[n_prompt + s * DEC_SEQ:n_prompt + (s + 1) * DEC_SEQ])], axis=0)
        kf, vf = _mla_kv(ckv_all, kpe_all, wuk[l], wuv[l])
        lk = PAST_LEN + DEC_SEQ
        yc_p = _attention(q[:n_prompt], kf[:n_prompt].reshape(BATCH, SEQ, -1),
                          vf[:n_prompt].reshape(BATCH, SEQ, -1), SEQ)
        yc_s = _attention(q[n_prompt:], kf[n_prompt:].reshape(DEC_BATCH, lk, -1),
                          vf[n_prompt:].reshape(DEC_BATCH, lk, -1), 256)
        yc = jnp.concatenate([yc_p, yc_s], axis=0)
        x1, u2, logits = _merge(x, rw, ml, yc, proj, modt, row2(rw_ln_w)[l], row2(rw_ln_b)[l],
                                row2(ml_ln_w)[l], row2(ml_ln_b)[l], pa[l], pb[l], pc[l], wo[l],
                                row2(ln1_w)[l], row2(ln1_b)[l], rtw[l], rtb[l])
        tok_sorted, w_sorted, inv, items = _route(logits[:, :N_EXPERTS])
        y_sorted = _moe_experts(l, u2[tok_sorted], items, w1s, b1g, b1l, moe_w2, b2, w_sorted[:, None])
        y_slots = y_sorted[inv.reshape(N_TOK, TOP_K).T.reshape(-1)].reshape(TOP_K, N_TOK, D_MODEL)
        x = _final_norm(x1, y_slots, modt, row2(ln2_w)[l], row2(ln2_b)[l])
        outs['rw'].append(jnp.swapaxes(jnp.stack([rw_fin[0][:BATCH], rw_fin[1][:BATCH]], axis=1), -1, -2))
        outs['mlc'].append(ml_fin[:BATCH])
        outs['ckv'].append(ckv[:n_prompt].reshape(BATCH, SEQ, C_KV_LORA))
        outs['kpe'].append(kpe[:n_prompt, C_NOPE:C_NOPE + C_ROPE].reshape(BATCH, SEQ, C_ROPE))

    stack = lambda k: jnp.stack(outs[k], axis=1)
    mlc = stack('mlc')
    return (x[:n_prompt].reshape(BATCH, SEQ, D_MODEL), x[n_prompt:].reshape(DEC_BATCH, DEC_SEQ, D_MODEL),
            stack('rw'), mlc[..., 0:B_DK], mlc[..., N_COL], mlc[..., 0, M_COL], stack('ckv'), stack('kpe'))
```

```python
import functools

import numpy as np
import jax
import jax.numpy as jnp
from jax import lax
from jax.experimental import pallas as pl
from jax.experimental.pallas import tpu as pltpu

F32 = jnp.float32
BF16 = jnp.bfloat16

D_MODEL = 1024
BATCH = 16
SEQ = 256
DEPTH = 4
DEC_BATCH = 2
DEC_SEQ = 4096
PAST_LEN = 256
GRID_W = 64

A_HEADS = 4
A_HEAD = 64
A_W = 256
A_GN_EPS = 64e-5
B_HEADS = 4
B_DK = 64
B_W = 256
C_HEADS = 8
C_NOPE = 64
C_ROPE = 32
C_V = 64
C_Q_LORA = 256
C_KV_LORA = 128
C_W = 512
ROPE_BASE = 10000.0
N_EXPERTS = 32
TOP_K = 4
D_EXPERT = 1024
SWIGLU_LIMIT = 7.0
SWIGLU_ALPHA = 1.702
MOE_BLOCK = 256
A_IN = 1024
B_IN = 1040
C_IN = 416
DN_ALPHA = (2 * DEPTH) ** 0.25
LN_EPS = 1e-5
RMS_EPS = 1e-6

TILE = 256
CHUNK = 64
N_TOK = BATCH * SEQ + DEC_BATCH * DEC_SEQ
N_TILES = N_TOK // TILE
N_PROMPT_TILES = BATCH * SEQ // TILE
TILES_PER_SAMPLE = DEC_SEQ // TILE
N_SEQ = BATCH + DEC_BATCH
HALO = 8

COL_A = 0
COL_G = 1024
COL_B = 4096
COL_C = 5120
COL_BG = 5632
N_PROJ = 5888

VMEM_LIMIT = 56 * 1024 * 1024


def _cparams(sem):
    return pltpu.CompilerParams(dimension_semantics=sem, vmem_limit_bytes=VMEM_LIMIT)


NN = ((1,), (0,))
NT = ((1,), (1,))
TN = ((0,), (0,))


def _dot(a, b, dims=NN):
    return lax.dot_general(a, b, (dims, ((), ())), preferred_element_type=F32)


def _split(x, n):
    parts, r = [], x
    for i in range(n):
        p = r.astype(BF16)
        parts.append(p)
        if i + 1 < n:
            r = r - p.astype(F32)
    return parts


def _mm(a, b, passes=1, dims=NN):
    if passes == 1:
        return _dot(a.astype(BF16), b.astype(BF16), dims)
    ah, al = _split(a, 2)
    bh, bl = _split(b, 2)
    return _dot(ah, bh, dims) + (_dot(al, bh, dims) + _dot(ah, bl, dims))


def _mm_exact_l(t01, x):
    x0, x1, x2 = _split(x, 3)
    return _dot(t01, x0) + (_dot(t01, x1) + _dot(t01, x2))


def _mm_exact_r(x, t01):
    x0, x1, x2 = _split(x, 3)
    return _dot(x0, t01) + (_dot(x1, t01) + _dot(x2, t01))


def _sigmoid(x):
    return 1.0 / (1.0 + jnp.exp(-x))


def _softplus(x):
    return jnp.maximum(x, 0.0) + jnp.log(1.0 + jnp.exp(-jnp.abs(x)))


def _block_ones(n, blk):
    r = lax.broadcasted_iota(jnp.int32, (n, n), 0) // blk
    c = lax.broadcasted_iota(jnp.int32, (n, n), 1) // blk
    return jnp.where(r == c, 1.0, 0.0).astype(BF16)


def _layer_norm(x, w, b):
    mu = jnp.mean(x, axis=-1, keepdims=True)
    xc = x - mu
    var = jnp.mean(xc * xc, axis=-1, keepdims=True)
    return xc * lax.rsqrt(var + LN_EPS) * w + b


def _head_norm(x, ones_blk, w, b, eps):
    mu = _mm_exact_r(x, ones_blk) * (1.0 / 64.0)
    xc = x - mu
    var = _mm_exact_r(xc * xc, ones_blk) * (1.0 / 64.0)
    return xc * lax.rsqrt(var + eps) * w + b


def _shifted(x, prev_row, next_row):
    rid = lax.broadcasted_iota(jnp.int32, x.shape, 0)
    xp = jnp.where(rid == 0, prev_row, pltpu.roll(x, 1, axis=0))
    xn = jnp.where(rid == x.shape[0] - 1, next_row, pltpu.roll(x, x.shape[0] - 1, axis=0))
    return xp, xn


def _schedule():
    seq_of_tile = np.concatenate([np.arange(BATCH), BATCH + np.repeat(np.arange(DEC_BATCH), TILES_PER_SAMPLE)])
    first_of_tile = np.ones(N_TILES, np.int32)
    last_of_tile = np.ones(N_TILES, np.int32)
    for s in range(DEC_BATCH):
        base = N_PROMPT_TILES + s * TILES_PER_SAMPLE
        first_of_tile[base + 1: base + TILES_PER_SAMPLE] = 0
        last_of_tile[base: base + TILES_PER_SAMPLE - 1] = 0
    tile_fwd = np.arange(N_TILES)
    tile_bwd = np.arange(N_TILES)
    for s in range(DEC_BATCH):
        base = N_PROMPT_TILES + s * TILES_PER_SAMPLE
        tile_bwd[base: base + TILES_PER_SAMPLE] = base + TILES_PER_SAMPLE - 1 - np.arange(TILES_PER_SAMPLE)
    tile = np.concatenate([tile_fwd, tile_bwd]).astype(np.int32)
    seq = seq_of_tile[tile].astype(np.int32)
    begins = np.concatenate([first_of_tile[tile_fwd], last_of_tile[tile_bwd]]).astype(np.int32)
    ends = np.concatenate([last_of_tile[tile_fwd], first_of_tile[tile_bwd]]).astype(np.int32)
    has_prev = (1 - first_of_tile).astype(np.int32)
    has_next = (1 - last_of_tile).astype(np.int32)
    return tile, seq, begins, ends, has_prev, has_next


_SCHED = _schedule()
_SEQ_OF_TILE = np.concatenate([np.arange(BATCH), BATCH + np.repeat(np.arange(DEC_BATCH), TILES_PER_SAMPLE)]).astype(np.int32)
_MOD_ROW_OF_TILE = np.concatenate([np.zeros(N_PROMPT_TILES), 1 + np.repeat(np.arange(DEC_BATCH), TILES_PER_SAMPLE)]).astype(np.int32)


def _direction_masks(d):
    row = lax.broadcasted_iota(jnp.int32, (CHUNK, CHUNK), 0)
    col = lax.broadcasted_iota(jnp.int32, (CHUNK, CHUNK), 1)
    diff = (row - col) * (1 - 2 * d)
    return diff >= 0, diff > 0, diff <= 0


def _mod_kernel(c_ref, w_ref, b_ref, o_ref):
    c = c_ref[...]
    o_ref[0] = _mm(c * _sigmoid(c), w_ref[0], 3) + b_ref[0]


def _modulation(cond, ada_w, ada_b):
    tn = 1536
    return pl.pallas_call(
        _mod_kernel,
        grid=(DEPTH, 6 * D_MODEL // tn),
        in_specs=[pl.BlockSpec((8, D_MODEL), lambda l, j: (0, 0)),
                  pl.BlockSpec((1, D_MODEL, tn), lambda l, j: (l, 0, j)),
                  pl.BlockSpec((1, 1, tn), lambda l, j: (l, 0, j))],
        out_specs=pl.BlockSpec((1, 8, tn), lambda l, j: (l, 0, j)),
        out_shape=jax.ShapeDtypeStruct((DEPTH, 8, 6 * D_MODEL), F32),
        compiler_params=_cparams(("parallel", "parallel")),
        name="ada_modulation",
    )(cond, ada_w, ada_b.reshape(DEPTH, 1, 6 * D_MODEL))


IN_TM = 512
IN_TN = N_PROJ // 2


def _inproj_kernel(x_ref, sh_ref, sc_ref, w_ref, o_ref):
    u = x_ref[...] * (1.0 + sc_ref[0]) + sh_ref[0]
    o_ref[...] = _dot(u.astype(BF16), w_ref[...])


def _in_projection(x, modt, w):
    rep = IN_TM // TILE
    return pl.pallas_call(
        _inproj_kernel,
        grid=(N_PROJ // IN_TN, N_TOK // IN_TM),
        in_specs=[pl.BlockSpec((IN_TM, D_MODEL), lambda j, i: (i, 0)),
                  pl.BlockSpec((1, 1, D_MODEL), lambda j, i: (rep * i, 0, 0)),
                  pl.BlockSpec((1, 1, D_MODEL), lambda j, i: (rep * i, 0, 1)),
                  pl.BlockSpec((D_MODEL, IN_TN), lambda j, i: (0, j))],
        out_specs=pl.BlockSpec((IN_TM, IN_TN), lambda j, i: (i, j)),
        out_shape=jax.ShapeDtypeStruct((N_TOK, N_PROJ), F32),
        compiler_params=_cparams(("parallel", "parallel")),
        name="in_projection",
    )(x, modt, modt, w)


RW_P = 1
RW_GRAM_P = 1
RW_UPD_P = 1
RW_INV_P = 1
RW_ST_P = 3
HB = A_HEADS * CHUNK


def _pre(x, passes):
    return tuple(_split(x, 2)) if passes == 3 else (x.astype(BF16),)


def _mmp(a, b, dims=NN):
    out = _dot(a[0], b[0], dims)
    if len(a) == 2 and len(b) == 2:
        out = out + (_dot(a[1], b[0], dims) + _dot(a[0], b[1], dims))
    return out


def _head_expand(m, same_head):
    return jnp.where(same_head, jnp.concatenate([m] * A_HEADS, axis=0), 0.0)


def _head_stack(m):
    return jnp.concatenate([m[:, h * A_HEAD:(h + 1) * A_HEAD] for h in range(A_HEADS)], axis=0)


def _head_unstack(m):
    return jnp.concatenate([m[h * CHUNK:(h + 1) * CHUNK, :] for h in range(A_HEADS)], axis=1)


def _rwkv_kernel(d, tile_s, seq_s, begin_s, end_s, hasprev_s, hasnext_s,
                 xa_ref, xp_ref, xn_ref, s0_ref, mu_ref, w0_ref, w2_ref, a0_ref, a2_ref, g2_ref,
                 kk_ref, ka_ref, rk_ref,
                 out_ref, sfin_ref,
                 st_scr):
    step = pl.program_id(0)
    tile = tile_s[step]

    @pl.when(begin_s[step] == 1)
    def _():
        st_scr[...] = s0_ref[0]

    x = xa_ref[...]
    prev_row = xp_ref[HALO - 1:HALO, :] * hasprev_s[tile].astype(F32)
    next_row = xn_ref[0:1, :] * hasnext_s[tile].astype(F32)
    xp, xn = _shifted(x, prev_row, next_row)
    xs = x + mu_ref[...] * (0.5 * (xp + xn) - x)
    r = xs[:, 0:256]
    k = xs[:, 256:512]
    v = xs[:, 512:768]
    wl = xs[:, 768:832]
    al = xs[:, 832:896]
    gl = xs[:, 896:1024]

    ones_blk = _block_ones(A_W, A_HEAD)
    kkv = k * kk_ref[...]
    nrm = jnp.sqrt(_mm_exact_r(kkv * kkv, ones_blk))
    kap = kkv / jnp.maximum(nrm, 1e-6)
    wpre = w0_ref[...] + _mm(jnp.tanh(wl), w2_ref[...], 3)
    lw = -jnp.exp(-_softplus(-wpre) - 0.5)
    a = _sigmoid(a0_ref[...] + _mm(al, a2_ref[...], 3))
    kd = k * (1.0 + (a - 1.0) * ka_ref[...])
    bonus = _mm_exact_r(r * kd * rk_ref[...], ones_blk) * v
    out_ref[:, 256:512] = bonus
    out_ref[:, 512:768] = _mm(_sigmoid(gl), g2_ref[...], 3)

    b = kap * a

    incl64, _, _ = _direction_masks(d)
    tinc = jnp.where(incl64, 1.0, 0.0).astype(BF16)
    row = lax.broadcasted_iota(jnp.int32, (HB, HB), 0)
    col = lax.broadcasted_iota(jnp.int32, (HB, HB), 1)
    same_head = (row // CHUNK) == (col // CHUNK)
    order = jnp.where(same_head, (row % CHUNK - col % CHUNK) * (1 - 2 * d), -1)
    incl = order >= 0
    strict = order > 0
    eye = jnp.where(row == col, 1.0, 0.0)
    xor = row ^ col
    level_masks = [(xor >> s) == 1 for s in range(CHUNK.bit_length() - 1)]
    n_chunks = TILE // CHUNK
    chunks = [c if d == 0 else n_chunks - 1 - c for c in range(n_chunks)]
    sl = [slice(c * CHUNK, (c + 1) * CHUNK) for c in chunks]
    rng = range(n_chunks)

    lwc = [lw[s] for s in sl]
    cs = [_mm_exact_l(tinc, x) for x in lwc]
    w_in = [jnp.exp(x) for x in cs]
    w_ex = [jnp.exp(cs[i] - lwc[i]) for i in rng]
    w_inv = [jnp.exp(-x) for x in cs]
    w_tot = [jnp.exp(jnp.sum(x, axis=0, keepdims=True)) for x in lwc]
    kt = [_head_expand(kap[sl[i]] * w_ex[i], same_head) for i in rng]
    rt = [_head_expand(r[sl[i]] * w_in[i], same_head) for i in rng]
    bt = [b[sl[i]] * w_inv[i] for i in rng]
    kdt = [kd[sl[i]] * w_inv[i] for i in rng]
    v_stack = [_head_stack(v[s]) for s in sl]
    v_st = [_pre(x, RW_P) for x in v_stack]
    v_su = v_st if RW_UPD_P == RW_P else [_pre(x, RW_UPD_P) for x in v_stack]
    gram = [_mm(jnp.concatenate([kt[i], rt[i]], axis=0),
                jnp.concatenate([bt[i]] * A_HEADS + [kdt[i]] * A_HEADS, axis=0), RW_GRAM_P, NT) for i in rng]
    l_b = [jnp.where(strict, g[:HB, :HB], 0.0) for g in gram]
    l_k = [jnp.where(strict, g[:HB, HB:], 0.0) for g in gram]
    m_b = [_pre(jnp.where(incl, g[HB:, :HB], 0.0), RW_P) for g in gram]
    m_k = [jnp.where(incl, g[HB:, HB:], 0.0) for g in gram]
    t_inv = [eye - jnp.where(level_masks[0], x, 0.0) for x in l_b]
    for mask in level_masks[1:]:
        tp = [_pre(t, RW_INV_P) for t in t_inv]
        w1 = [_mmp(tp[i], _pre(jnp.where(mask, l_b[i], 0.0), RW_INV_P)) for i in rng]
        t_inv = [t_inv[i] - _mmp(_pre(w1[i], RW_INV_P), tp[i]) for i in rng]
    lkv = [_mmp(_pre(l_k[i], RW_P), v_st[i]) for i in rng]
    xx = [_mm(t_inv[i], jnp.concatenate([kt[i], lkv[i]], axis=1), RW_P) for i in rng]
    xxp = [_pre(x, RW_P) for x in xx]
    mx = [_mmp(m_b[i], xxp[i]) for i in rng]
    rhat = [_pre(rt[i] - mx[i][:, :HB], RW_ST_P) for i in rng]
    y0 = [_mmp(_pre(m_k[i], RW_P), v_st[i]) - mx[i][:, HB:] for i in rng]
    xxu = xxp if RW_UPD_P == RW_P else [_pre(x, RW_UPD_P) for x in xx]
    bhp = [_pre(_head_expand(bt[i] * w_tot[i], same_head), RW_UPD_P) for i in rng]
    bx = [_mmp(bhp[i], xxu[i], TN) for i in rng]
    g_mat = [_pre(eye * w_tot[i] - bx[i][:, :HB], RW_ST_P) for i in rng]
    h_mat = [_mmp(_pre(_head_expand(kdt[i] * w_tot[i], same_head), RW_UPD_P), v_su[i], TN) - bx[i][:, HB:]
             for i in rng]

    st = st_scr[...]
    for i in rng:
        stp = _pre(st, RW_ST_P)
        out_ref[sl[i], 0:256] = _head_unstack(_mmp(rhat[i], stp) + y0[i])
        st = _mmp(g_mat[i], stp) + h_mat[i]
    st_scr[...] = st

    @pl.when(end_s[step] == 1)
    def _():
        sfin_ref[0] = st


def _rwkv_direction(d, proj, s0, mu, w0, w2, a0, a2, g2, kk, ka, rk):
    nrb = N_TOK // HALO
    per = TILE // HALO
    tile, seq, begins, ends, has_prev, has_next = _SCHED
    half = slice(d * N_TILES, (d + 1) * N_TILES)
    sched = tuple(jnp.asarray(a) for a in (tile[half], seq[half], begins[half], ends[half], has_prev, has_next))

    def const(shape):
        return pl.BlockSpec(shape, lambda i, *_: (0,) * len(shape))

    grid_spec = pltpu.PrefetchScalarGridSpec(
        num_scalar_prefetch=6,
        grid=(N_TILES,),
        in_specs=[
            pl.BlockSpec((TILE, A_IN), lambda i, t, *_: (t[i], COL_A // A_IN)),
            pl.BlockSpec((HALO, A_IN), lambda i, t, *_: (jnp.maximum(t[i] * per - 1, 0), 0)),
            pl.BlockSpec((HALO, A_IN), lambda i, t, *_: (jnp.minimum(t[i] * per + per, nrb - 1), 0)),
            pl.BlockSpec((1, HB, A_HEAD), lambda i, t, s, *_: (s[i], 0, 0)),
            const((1, A_IN)),
            const((1, A_W)), const((64, A_W)), const((1, A_W)), const((64, A_W)),
            const((128, A_W)), const((1, A_W)), const((1, A_W)), const((1, A_W)),
        ],
        out_specs=[
            pl.BlockSpec((TILE, 768), lambda i, t, *_: (t[i], 0)),
            pl.BlockSpec((1, HB, A_HEAD), lambda i, t, s, *_: (s[i], 0, 0)),
        ],
        scratch_shapes=[pltpu.VMEM((HB, A_HEAD), F32)],
    )
    return pl.pallas_call(
        functools.partial(_rwkv_kernel, d),
        grid_spec=grid_spec,
        out_shape=[jax.ShapeDtypeStruct((N_TOK, 768), F32),
                   jax.ShapeDtypeStruct((N_SEQ, HB, A_HEAD), F32)],
        compiler_params=_cparams(("arbitrary",)),
        name="rwkv7_mixer_fwd" if d == 0 else "rwkv7_mixer_bwd",
    )(*sched, proj, proj, proj, s0, mu, w0, w2, a0, a2, g2, kk, ka, rk)


def _rwkv(proj, s0, mu, w0, w2, a0, a2, g2, kk, ka, rk):
    outs = [_rwkv_direction(d, proj, s0[:, d].reshape(N_SEQ, HB, A_HEAD), mu, w0[d], w2[d], a0[d], a2[d],
                            g2, kk, ka, rk) for d in range(2)]
    fin = tuple(o[1].reshape(N_SEQ, A_HEADS, A_HEAD, A_HEAD) for o in outs)
    return (outs[0][0], outs[1][0]), fin


ML_P = 1
N_COL = B_DK
M_COL = B_DK + 1
NEG = -1e30


def _per_head_col(x, first):
    return jnp.concatenate([x[:, first + h:first + h + 1] for h in range(B_HEADS)], axis=0)


def _per_head_row(x_t, first):
    return jnp.concatenate([x_t[first + h:first + h + 1, :] for h in range(B_HEADS)], axis=1)


def _head_fill(x, first):
    return jnp.concatenate([jnp.broadcast_to(x[:, first + h:first + h + 1], (CHUNK, 1)) for h in range(B_HEADS)],
                           axis=0)


def _head_max(x):
    return jnp.concatenate(
        [jnp.broadcast_to(jnp.max(x[h * CHUNK:(h + 1) * CHUNK], axis=0, keepdims=True), (CHUNK, 1))
         for h in range(B_HEADS)], axis=0)


def _mlstm_kernel(d, tile_s, seq_s, begin_s, end_s, hasprev_s, hasnext_s,
                  xb_ref, xp_ref, xn_ref, gt_ref, s0_ref, cw_ref, gb_ref,
                  out_ref, sfin_ref,
                  cn_scr, m_scr):
    step = pl.program_id(0)
    tile = tile_s[step]

    @pl.when(begin_s[step] == 1)
    def _():
        s0 = s0_ref[0]
        cn_scr[...] = s0
        m_scr[...] = s0[:, M_COL:M_COL + 1]

    qk = xb_ref[:, 0:512]
    prev_row = xp_ref[HALO - 1:HALO, 0:512] * hasprev_s[tile].astype(F32)
    next_row = xn_ref[0:1, 0:512] * hasnext_s[tile].astype(F32)
    qp, qn = _shifted(qk, prev_row, next_row)
    conv = cw_ref[0:1, :] * qp + cw_ref[1:2, :] * qk + cw_ref[2:3, :] * qn
    act = conv * _sigmoid(conv)
    q = act[:, 0:256]
    k = act[:, 256:512] * (B_DK ** -0.5)
    v = xb_ref[:, 512:768]
    li = gt_ref[...] + gb_ref[...]
    lf = -_softplus(-li)

    incl64, _, incl64_t = _direction_masks(d)
    tinc = jnp.where(incl64, 1.0, 0.0).astype(BF16)
    tinc_t = jnp.where(incl64_t, 1.0, 0.0).astype(BF16)
    row = lax.broadcasted_iota(jnp.int32, (HB, HB), 0)
    col = lax.broadcasted_iota(jnp.int32, (HB, HB), 1)
    same_head = (row // CHUNK) == (col // CHUNK)
    causal = jnp.where(same_head, (row % CHUNK - col % CHUNK) * (1 - 2 * d), -1) >= 0
    lane = lax.broadcasted_iota(jnp.int32, (HB, 128), 1)
    n_chunks = TILE // CHUNK
    chunks = [c if d == 0 else n_chunks - 1 - c for c in range(n_chunks)]
    sl = [slice(c * CHUNK, (c + 1) * CHUNK) for c in chunks]
    rng = range(n_chunks)
    fg = B_HEADS

    li_c = [li[s] for s in sl]
    lf_c = [lf[s] for s in sl]
    b_c = [_mm_exact_l(tinc, x) for x in lf_c]
    b_r = [_mm_exact_r(x.T, tinc_t) for x in lf_c]
    li_r = [x.T for x in li_c]
    bc = [_per_head_col(x, fg) for x in b_c]
    lic = [_per_head_col(x, 0) for x in li_c]
    br = [_per_head_row(x, fg) for x in b_r]
    lir = [_per_head_row(x, 0) for x in li_r]
    be = [_head_fill(jnp.sum(x, axis=0, keepdims=True), fg) for x in lf_c]
    wlog = [be[i] - bc[i] + lic[i] for i in rng]
    wmax = [_head_max(x) for x in wlog]
    m_old, m_new = [], []
    m = m_scr[...]
    for i in rng:
        m_old.append(m)
        m = jnp.maximum(be[i] + m, wmax[i])
        m_new.append(m)
    m_scr[...] = m

    dlog = [jnp.where(causal, bc[i] - br[i] + lir[i], NEG) for i in rng]
    inter = [bc[i] + m_old[i] for i in rng]
    mj = [jnp.maximum(jnp.max(dlog[i], axis=1, keepdims=True), inter[i]) for i in rng]
    q_exp = [_pre(_head_expand(q[s], same_head), ML_P) for s in sl]
    k_exp = [_head_expand(k[s], same_head) for s in sl]
    s_mat = [_mmp(q_exp[i], _pre(jnp.concatenate([k[sl[i]]] * B_HEADS, axis=0), ML_P), NT)
             * jnp.exp(dlog[i] - mj[i]) for i in rng]
    e_int = [jnp.exp(inter[i] - mj[i]) for i in rng]
    v_aug = [jnp.where(lane == N_COL, 1.0,
                       jnp.concatenate([_head_stack(v[s]), jnp.zeros((HB, 128 - B_DK), F32)], axis=1)) for s in sl]
    sv = [_mm(s_mat[i], v_aug[i], ML_P) for i in rng]
    wk = [jnp.exp(wlog[i] - m_new[i]) for i in rng]
    dec = [jnp.exp(be[i] + m_old[i] - m_new[i]) for i in rng]
    kv = [_mm(k_exp[i], wk[i] * v_aug[i], ML_P, TN) for i in rng]
    floor = [jnp.exp(-x) for x in mj]

    cn = cn_scr[...]
    for i in rng:
        nd = sv[i] + e_int[i] * _mmp(q_exp[i], _pre(cn, ML_P))
        den = jnp.maximum(jnp.abs(nd[:, N_COL:N_COL + 1]), floor[i])
        out_ref[sl[i], :] = _head_unstack(nd[:, 0:B_DK] / den)
        cn = dec[i] * cn + kv[i]
    cn_scr[...] = cn

    @pl.when(end_s[step] == 1)
    def _():
        sfin_ref[0] = jnp.where(lane == M_COL, m, cn)


def _mlstm_direction(d, proj, s0, conv_w, gate_b):
    nrb = N_TOK // HALO
    per = TILE // HALO
    tile, seq, begins, ends, has_prev, has_next = _SCHED
    half = slice(d * N_TILES, (d + 1) * N_TILES)
    sched = tuple(jnp.asarray(a) for a in (tile[half], seq[half], begins[half], ends[half], has_prev, has_next))
    cb = COL_B // 1024
    grid_spec = pltpu.PrefetchScalarGridSpec(
        num_scalar_prefetch=6,
        grid=(N_TILES,),
        in_specs=[
            pl.BlockSpec((TILE, 1024), lambda i, t, *_: (t[i], cb)),
            pl.BlockSpec((HALO, 1024), lambda i, t, *_: (jnp.maximum(t[i] * per - 1, 0), cb)),
            pl.BlockSpec((HALO, 1024), lambda i, t, *_: (jnp.minimum(t[i] * per + per, nrb - 1), cb)),
            pl.BlockSpec((TILE, 128), lambda i, t, *_: (t[i], COL_BG // 128 + d)),
            pl.BlockSpec((1, HB, 128), lambda i, t, s, *_: (s[i], 0, 0)),
            pl.BlockSpec((3, 512), lambda i, *_: (0, 0)),
            pl.BlockSpec((1, 128), lambda i, *_: (0, 0)),
        ],
        out_specs=[
            pl.BlockSpec((TILE, B_W), lambda i, t, *_: (t[i], 0)),
            pl.BlockSpec((1, HB, 128), lambda i, t, s, *_: (s[i], 0, 0)),
        ],
        scratch_shapes=[pltpu.VMEM((HB, 128), F32), pltpu.VMEM((HB, 1), F32)],
    )
    return pl.pallas_call(
        functools.partial(_mlstm_kernel, d),
        grid_spec=grid_spec,
        out_shape=[jax.ShapeDtypeStruct((N_TOK, B_W), F32),
                   jax.ShapeDtypeStruct((N_SEQ, HB, 128), F32)],
        compiler_params=_cparams(("arbitrary",)),
        name="mlstm_mixer_fwd" if d == 0 else "mlstm_mixer_bwd",
    )(*sched, proj, proj, proj, proj, s0, conv_w, gate_b)


def _mlstm(proj, s0, conv_w, gate_b):
    outs = [_mlstm_direction(d, proj, s0[:, d].reshape(N_SEQ, HB, 128), conv_w, gate_b[d]) for d in range(2)]
    fin = jnp.stack([o[1].reshape(N_SEQ, B_HEADS, B_DK, 128) for o in outs], axis=1)
    return (outs[0][0], outs[1][0]), fin


HEAD_G = 128
ATT_SCALE = (C_NOPE + C_ROPE) ** -0.5


def _rope(x, cos, sin_lo, sin_hi):
    return x * cos + pltpu.roll(x, 16, axis=1) * sin_hi + pltpu.roll(x, HEAD_G - 16, axis=1) * sin_lo


def _mla_pre_kernel(xc_ref, qn_ref, kvn_ref, wuq_ref, cos_ref, slo_ref, shi_ref, q_ref, ckv_ref, kpe_ref):
    q_dn = xc_ref[:, 0:C_Q_LORA]
    qn = q_dn * lax.rsqrt(jnp.mean(q_dn * q_dn, axis=-1, keepdims=True) + RMS_EPS) * qn_ref[...]
    q = _dot(qn.astype(BF16), wuq_ref[...])
    cos, slo, shi = cos_ref[...], slo_ref[...], shi_ref[...]
    for h in range(C_HEADS):
        sl = slice(h * HEAD_G, (h + 1) * HEAD_G)
        q_ref[:, sl] = (_rope(q[:, sl], cos, slo, shi) * ATT_SCALE).astype(BF16)
    kv_dn = xc_ref[:, C_Q_LORA:C_Q_LORA + C_KV_LORA]
    ckv_ref[...] = kv_dn * lax.rsqrt(jnp.mean(kv_dn * kv_dn, axis=-1, keepdims=True) + RMS_EPS) * kvn_ref[...]
    kpe_ref[...] = _rope(xc_ref[:, 384:512], cos, slo, shi)


def _mla_pre(proj, q_norm, kv_norm, wuq, rope_tabs):
    cc = COL_C // 512

    def tab_idx(i):
        return (jnp.where(i < N_PROMPT_TILES, TILES_PER_SAMPLE, (i - N_PROMPT_TILES) % TILES_PER_SAMPLE), 0)

    tab_spec = pl.BlockSpec((TILE, HEAD_G), tab_idx)
    return pl.pallas_call(
        _mla_pre_kernel,
        grid=(N_TILES,),
        in_specs=[pl.BlockSpec((TILE, 512), lambda i: (i, cc)),
                  pl.BlockSpec((1, C_Q_LORA), lambda i: (0, 0)),
                  pl.BlockSpec((1, C_KV_LORA), lambda i: (0, 0)),
                  pl.BlockSpec((C_Q_LORA, C_HEADS * HEAD_G), lambda i: (0, 0)),
                  tab_spec, tab_spec, tab_spec],
        out_specs=[pl.BlockSpec((TILE, C_HEADS * HEAD_G), lambda i: (i, 0)),
                   pl.BlockSpec((TILE, C_KV_LORA), lambda i: (i, 0)),
                   pl.BlockSpec((TILE, HEAD_G), lambda i: (i, 0))],
        out_shape=[jax.ShapeDtypeStruct((N_TOK, C_HEADS * HEAD_G), BF16),
                   jax.ShapeDtypeStruct((N_TOK, C_KV_LORA), F32),
                   jax.ShapeDtypeStruct((N_TOK, HEAD_G), F32)],
        compiler_params=_cparams(("parallel",)),
        name="mla_pre",
    )(proj, q_norm, kv_norm, wuq, *rope_tabs)


def _mla_kv_kernel(ckv_ref, kpe_ref, wuk_ref, wuv_ref, k_ref, v_ref):
    ckv = ckv_ref[...].astype(BF16)
    kn = _dot(ckv, wuk_ref[...])
    kpe = kpe_ref[...]
    for h in range(C_HEADS):
        sl = slice(h * HEAD_G, (h + 1) * HEAD_G)
        k_ref[:, sl] = (kn[:, sl] + kpe).astype(BF16)
    v_ref[...] = _dot(ckv, wuv_ref[...]).astype(BF16)


def _mla_kv(ckv_all, kpe_all, wuk, wuv):
    n = ckv_all.shape[0]
    return pl.pallas_call(
        _mla_kv_kernel,
        grid=(n // TILE,),
        in_specs=[pl.BlockSpec((TILE, C_KV_LORA), lambda i: (i, 0)),
                  pl.BlockSpec((TILE, HEAD_G), lambda i: (i, 0)),
                  pl.BlockSpec((C_KV_LORA, C_HEADS * HEAD_G), lambda i: (0, 0)),
                  pl.BlockSpec((C_KV_LORA, C_W), lambda i: (0, 0))],
        out_specs=[pl.BlockSpec((TILE, C_HEADS * HEAD_G), lambda i: (i, 0)),
                   pl.BlockSpec((TILE, C_W), lambda i: (i, 0))],
        out_shape=[jax.ShapeDtypeStruct((n, C_HEADS * HEAD_G), BF16),
                   jax.ShapeDtypeStruct((n, C_W), BF16)],
        compiler_params=_cparams(("parallel",)),
        name="mla_kv",
    )(ckv_all, kpe_all, wuk, wuv)


def _attn_kernel(q_ref, k_ref, v_ref, o_ref):
    for h in range(C_HEADS):
        sl = slice(h * HEAD_G, (h + 1) * HEAD_G)
        s = _dot(q_ref[:, sl], k_ref[0, :, sl], NT)
        e = jnp.exp(s - jnp.max(s, axis=1, keepdims=True))
        den = jnp.sum(e, axis=1, keepdims=True)
        o = _dot(e.astype(BF16), v_ref[0, :, h * C_V:(h + 1) * C_V])
        o_ref[:, h * C_V:(h + 1) * C_V] = o / den


def _attention(q, k, v, tq):
    n_seq, lk, _ = k.shape
    lq = q.shape[0] // n_seq
    nqb = lq // tq
    return pl.pallas_call(
        _attn_kernel,
        grid=(n_seq, nqb),
        in_specs=[pl.BlockSpec((tq, C_HEADS * HEAD_G), lambda s, j: (s * nqb + j, 0)),
                  pl.BlockSpec((1, lk, C_HEADS * HEAD_G), lambda s, j: (s, 0, 0)),
                  pl.BlockSpec((1, lk, C_W), lambda s, j: (s, 0, 0))],
        out_specs=pl.BlockSpec((tq, C_W), lambda s, j: (s * nqb + j, 0)),
        out_shape=jax.ShapeDtypeStruct((q.shape[0], C_W), F32),
        compiler_params=_cparams(("parallel", "parallel")),
        name="mla_attention",
    )(q, k, v)


def _merge_kernel(x_ref, rw0_ref, rw1_ref, ml0_ref, ml1_ref, yc_ref, ga_ref, gb_ref, gc_ref, og_ref,
                  g1_ref, sh2_ref, sc2_ref, rwlw_ref, rwlb_ref, mllw_ref, mllb_ref,
                  pa_ref, pb_ref, pc_ref, wo_ref, l1w_ref, l1b_ref, rtw_ref, rtb_ref,
                  x1_ref, u2_ref, lg_ref):
    ones_blk = _block_ones(A_W, A_HEAD)
    rw0 = rw0_ref[...]
    rw1 = rw1_ref[...]
    ya = _head_norm(rw0[:, 0:256] + rw1[:, 0:256], ones_blk, rwlw_ref[...], rwlb_ref[...], A_GN_EPS)
    ya = (ya + rw0[:, 256:512] + rw1[:, 256:512]) * rw0[:, 512:768]
    yb = _head_norm(ml0_ref[...] + ml1_ref[...], ones_blk, mllw_ref[...], mllb_ref[...], LN_EPS)
    yb = yb * _sigmoid(og_ref[...])
    merged = (_sigmoid(ga_ref[...]) * _dot(ya.astype(BF16), pa_ref[...])
              + _sigmoid(gb_ref[...]) * _dot(yb.astype(BF16), pb_ref[...])
              + _sigmoid(gc_ref[...]) * _dot(yc_ref[...].astype(BF16), pc_ref[...]))
    mix = _dot(merged.astype(BF16), wo_ref[...])
    x1 = _layer_norm(DN_ALPHA * x_ref[...] + g1_ref[0] * mix, l1w_ref[...], l1b_ref[...])
    x1_ref[...] = x1
    u2 = x1 * (1.0 + sc2_ref[0]) + sh2_ref[0]
    u2_ref[...] = u2.astype(BF16)
    lg_ref[...] = _mm(u2, rtw_ref[...], 3) + rtb_ref[...]


def _merge(x, rw, ml, yc, proj, modt, rwlw, rwlb, mllw, mllb, pa, pb, pc, wo, l1w, l1b, rtw, rtb):
    gcol = COL_G // 1024

    def row(shape):
        return pl.BlockSpec(shape, lambda i: (0, 0))

    def mod(kk):
        return pl.BlockSpec((1, 1, D_MODEL), lambda i: (i, 0, kk))

    return pl.pallas_call(
        _merge_kernel,
        grid=(N_TILES,),
        in_specs=[pl.BlockSpec((TILE, D_MODEL), lambda i: (i, 0)),
                  pl.BlockSpec((TILE, 768), lambda i: (i, 0)),
                  pl.BlockSpec((TILE, 768), lambda i: (i, 0)),
                  pl.BlockSpec((TILE, B_W), lambda i: (i, 0)),
                  pl.BlockSpec((TILE, B_W), lambda i: (i, 0)),
                  pl.BlockSpec((TILE, C_W), lambda i: (i, 0)),
                  pl.BlockSpec((TILE, D_MODEL), lambda i: (i, gcol)),
                  pl.BlockSpec((TILE, D_MODEL), lambda i: (i, gcol + 1)),
                  pl.BlockSpec((TILE, D_MODEL), lambda i: (i, gcol + 2)),
                  pl.BlockSpec((TILE, B_W), lambda i: (i, (COL_B + 768) // B_W)),
                  mod(2), mod(3), mod(4),
                  row((1, A_W)), row((1, A_W)), row((1, B_W)), row((1, B_W)),
                  row((A_W, D_MODEL)), row((B_W, D_MODEL)), row((C_W, D_MODEL)), row((D_MODEL, D_MODEL)),
                  row((1, D_MODEL)), row((1, D_MODEL)), row((D_MODEL, 128)), row((1, 128))],
        out_specs=[pl.BlockSpec((TILE, D_MODEL), lambda i: (i, 0)),
                   pl.BlockSpec((TILE, D_MODEL), lambda i: (i, 0)),
                   pl.BlockSpec((TILE, 128), lambda i: (i, 0))],
        out_shape=[jax.ShapeDtypeStruct((N_TOK, D_MODEL), F32),
                   jax.ShapeDtypeStruct((N_TOK, D_MODEL), BF16),
                   jax.ShapeDtypeStruct((N_TOK, 128), F32)],
        compiler_params=_cparams(("parallel",)),
        name="merge_postnorm_router",
    )(x, rw[0], rw[1], ml[0], ml[1], yc, proj, proj, proj, proj, modt, modt, modt,
      rwlw, rwlb, mllw, mllb, pa, pb, pc, wo, l1w, l1b, rtw, rtb)


N_ASSIGN = N_TOK * TOP_K
N_ROW_BLOCKS = N_ASSIGN // MOE_BLOCK
N_ITEMS = N_ROW_BLOCKS + N_EXPERTS


def _moe_kernel(blk_s, exp_s, lo_s, hi_s, init_s, x_ref, w1g_ref, w1l_ref, b1g_ref, b1l_ref, w2_ref, b2_ref,
                rw_ref, y_ref):
    w = pl.program_id(0)

    @pl.when(init_s[w] == 1)
    def _():
        y_ref[...] = jnp.zeros_like(y_ref)

    @pl.when(hi_s[w] > lo_s[w])
    def _():
        x = x_ref[...]
        hg = jnp.minimum(_dot(x, w1g_ref[0, 0, 0]) + b1g_ref[0, 0], SWIGLU_LIMIT)
        hl = jnp.clip(_dot(x, w1l_ref[0, 0, 0]) + b1l_ref[0, 0], -SWIGLU_LIMIT, SWIGLU_LIMIT)
        act = hg * _sigmoid(SWIGLU_ALPHA * hg) * (hl + 1.0)
        y = _dot(act.astype(BF16), w2_ref[0, 0].astype(BF16)) + b2_ref[0, 0]
        rid = lax.broadcasted_iota(jnp.int32, (MOE_BLOCK, 1), 0)
        mine = (rid >= lo_s[w]) & (rid < hi_s[w])
        y_ref[...] = (y_ref[...].astype(F32) + jnp.where(mine, y * rw_ref[...], 0.0)).astype(y_ref.dtype)


def _moe_experts(l, x_sorted, items, w1s, b1g, b1l, w2, b2, w_sorted):
    def wspec(shape):
        return pl.BlockSpec((1, 1) + shape, lambda i, blk, ex, *_: (l, ex[i], 0, 0))

    def w1spec(half):
        return pl.BlockSpec((1, 1, 1, D_MODEL, D_EXPERT), lambda i, blk, ex, *_: (l, ex[i], half, 0, 0))

    grid_spec = pltpu.PrefetchScalarGridSpec(
        num_scalar_prefetch=5,
        grid=(N_ITEMS,),
        in_specs=[pl.BlockSpec((MOE_BLOCK, D_MODEL), lambda i, blk, *_: (blk[i], 0)),
                  w1spec(0), w1spec(1),
                  wspec((1, D_EXPERT)), wspec((1, D_EXPERT)),
                  wspec((D_EXPERT, D_MODEL)), wspec((1, D_MODEL)),
                  pl.BlockSpec((MOE_BLOCK, 1), lambda i, blk, *_: (blk[i], 0))],
        out_specs=pl.BlockSpec((MOE_BLOCK, D_MODEL), lambda i, blk, *_: (blk[i], 0)),
    )
    return pl.pallas_call(
        _moe_kernel,
        grid_spec=grid_spec,
        out_shape=jax.ShapeDtypeStruct((N_ASSIGN, D_MODEL), BF16),
        compiler_params=_cparams(("arbitrary",)),
        name="moe_experts",
    )(*items, x_sorted, w1s, w1s, b1g, b1l, w2, b2, w_sorted)


def _final_kernel(x_ref, f_ref, g2_ref, w_ref, b_ref, o_ref):
    ffn = (f_ref[0].astype(F32) + f_ref[1].astype(F32)) + (f_ref[2].astype(F32) + f_ref[3].astype(F32))
    o_ref[...] = _layer_norm(DN_ALPHA * x_ref[...] + g2_ref[0] * ffn, w_ref[...], b_ref[...])


def _final_norm(x1, y_slots, modt, w, b):
    return pl.pallas_call(
        _final_kernel,
        grid=(N_TILES,),
        in_specs=[pl.BlockSpec((TILE, D_MODEL), lambda i: (i, 0)),
                  pl.BlockSpec((TOP_K, TILE, D_MODEL), lambda i: (0, i, 0)),
                  pl.BlockSpec((1, 1, D_MODEL), lambda i: (i, 0, 5)),
                  pl.BlockSpec((1, D_MODEL), lambda i: (0, 0)),
                  pl.BlockSpec((1, D_MODEL), lambda i: (0, 0))],
        out_specs=pl.BlockSpec((TILE, D_MODEL), lambda i: (i, 0)),
        out_shape=jax.ShapeDtypeStruct((N_TOK, D_MODEL), F32),
        compiler_params=_cparams(("parallel",)),
        name="ffn_postnorm",
    )(x1, y_slots, modt, w, b)


def _route(logits):
    top_v, top_e = lax.top_k(logits, TOP_K)
    top_w = jax.nn.softmax(top_v, axis=-1)
    flat_e = top_e.reshape(-1).astype(jnp.int32)
    idx = jnp.arange(N_ASSIGN, dtype=jnp.int32)
    _, order, w_sorted, tok_sorted = lax.sort((flat_e, idx, top_w.reshape(-1), idx // TOP_K), num_keys=1)
    _, inv = lax.sort((order, idx), num_keys=1)
    ex = jnp.arange(N_EXPERTS, dtype=jnp.int32)
    counts = jnp.sum((flat_e[:, None] == ex[None, :]).astype(jnp.int32), axis=0)
    end = jnp.cumsum(counts)
    start = end - counts
    first_blk = start // MOE_BLOCK
    n_items = jnp.where(counts > 0, (end - 1) // MOE_BLOCK - first_blk + 1, 0)
    item_end = jnp.cumsum(n_items)
    item_start = item_end - n_items
    w = jnp.arange(N_ITEMS, dtype=jnp.int32)
    valid = w < item_end[-1]
    e_w = jnp.minimum(jnp.sum((item_end[None, :] <= w[:, None]).astype(jnp.int32), axis=1), N_EXPERTS - 1)
    pick = (e_w[:, None] == ex[None, :]).astype(jnp.int32)
    look = lambda tab: jnp.sum(pick * tab[None, :], axis=1)
    blk = jnp.where(valid, look(first_blk) + w - look(item_start), N_ROW_BLOCKS - 1)
    lo = jnp.where(valid, jnp.maximum(look(start) - blk * MOE_BLOCK, 0), 0)
    hi = jnp.where(valid, jnp.minimum(look(end) - blk * MOE_BLOCK, MOE_BLOCK), 0)
    e_last = jnp.max(jnp.where(counts > 0, ex, 0))
    e_w = jnp.where(valid, e_w, e_last)
    init = jnp.concatenate([jnp.ones((1,), jnp.int32), (blk[1:] != blk[:-1]).astype(jnp.int32)])
    items = tuple(a.astype(jnp.int32) for a in (blk, e_w, lo, hi, init))
    return tok_sorted, w_sorted, inv, items


def _rope_tables():
    rows = DEC_SEQ // GRID_W
    r, col = jnp.meshgrid(jnp.arange(rows, dtype=F32), jnp.arange(GRID_W, dtype=F32), indexing='ij')
    n_freq = C_ROPE // 4
    inv = 1.0 / (ROPE_BASE ** (jnp.arange(n_freq, dtype=F32) / n_freq))
    ang = jnp.concatenate([r.reshape(-1, 1) * inv, col.reshape(-1, 1) * inv], axis=-1)
    cos, sin = jnp.cos(ang), jnp.sin(ang)
    half = C_ROPE // 2
    one = jnp.ones((DEC_SEQ, C_NOPE), F32)
    zero = jnp.zeros((DEC_SEQ, C_NOPE), F32)
    tail1 = jnp.ones((DEC_SEQ, HEAD_G - C_NOPE - C_ROPE), F32)
    tail0 = jnp.zeros((DEC_SEQ, HEAD_G - C_NOPE - C_ROPE), F32)
    zh = jnp.zeros((DEC_SEQ, half), F32)
    t_cos = jnp.concatenate([one, cos, cos, tail1], axis=1)
    t_lo = jnp.concatenate([zero, -sin, zh, tail0], axis=1)
    t_hi = jnp.concatenate([zero, zh, sin, tail0], axis=1)
    ident = jnp.ones((TILE, HEAD_G), F32)
    nil = jnp.zeros((TILE, HEAD_G), F32)
    return (jnp.concatenate([t_cos, ident]), jnp.concatenate([t_lo, nil]), jnp.concatenate([t_hi, nil]))


def _pad_heads(w, n_heads, width):
    lead = w.shape[:-1]
    w = w.reshape(lead + (n_heads, width))
    w = jnp.pad(w, [(0, 0)] * len(lead) + [(0, 0), (0, HEAD_G - width)])
    return w.reshape(lead + (n_heads * HEAD_G,))


def kernel(x_prompt, x_sample, state_rwkv, state_mlstm_c, state_mlstm_n, state_mlstm_m, cache_mla_ckv,
           cache_mla_kpe, c, c_ctx, ada_w, ada_b, w_in, rw_mu, rw_w0, rw_w2, rw_a0, rw_a2, rw_g2, rw_kk,
           rw_ka, rw_rk, rw_ln_w, rw_ln_b, ml_conv, ml_gate_b, ml_ln_w, ml_ln_b, mla_q_norm, mla_wuq,
           mla_kv_norm, mla_wuk, mla_wuv, proj_a, proj_b, proj_c, w_out, ln1_w, ln1_b, router_w, router_b,
           moe_w1, moe_b1, moe_w2, moe_b2, ln2_w, ln2_b):
    L = DEPTH
    x = jnp.concatenate([x_prompt.reshape(-1, D_MODEL), x_sample.reshape(-1, D_MODEL)], axis=0)

    cond = jnp.concatenate([c_ctx[None], c, jnp.zeros((8 - 1 - DEC_BATCH, D_MODEL), F32)], axis=0)
    wa = w_in[:, :, 0:A_IN]
    wb = w_in[:, :, A_IN:A_IN + 1024]
    wbg = w_in[:, :, A_IN + 1024:A_IN + B_IN]
    wc = w_in[:, :, A_IN + B_IN:A_IN + B_IN + C_IN]
    wg = w_in[:, :, A_IN + B_IN + C_IN:]
    z = lambda n: jnp.zeros((L, D_MODEL, n), F32)
    w_proj = jnp.concatenate([
        wa, wg, wb,
        wc[:, :, 0:384], z(64), wc[:, :, 384:416], z(32),
        wbg[:, :, 0:8], z(120), wbg[:, :, 8:16], z(120)], axis=2).astype(BF16)
    gate_b = jnp.pad(ml_gate_b.reshape(L, 2, 1, 2 * B_HEADS), ((0, 0), (0, 0), (0, 0), (0, 128 - 2 * B_HEADS)))
    wuq = _pad_heads(mla_wuq, C_HEADS, C_NOPE + C_ROPE).astype(BF16)
    wuk = _pad_heads(mla_wuk, C_HEADS, C_NOPE).astype(BF16)
    wuv = mla_wuv.astype(BF16)
    pa, pb, pc, wo = proj_a.astype(BF16), proj_b.astype(BF16), proj_c.astype(BF16), w_out.astype(BF16)
    rtw = jnp.pad(router_w, ((0, 0), (0, 0), (0, 128 - N_EXPERTS)))
    rtb = jnp.pad(router_b, ((0, 0), (0, 128 - N_EXPERTS))).reshape(L, 1, 128)
    w1 = moe_w1.reshape(L, N_EXPERTS, D_MODEL, D_EXPERT, 2)
    w1s = jnp.moveaxis(w1, -1, 2).astype(BF16)
    b1 = moe_b1.reshape(L, N_EXPERTS, 1, D_EXPERT, 2)
    b1g, b1l = b1[..., 0], b1[..., 1]
    b2 = moe_b2.reshape(L, N_EXPERTS, 1, D_MODEL)
    rope_tabs = _rope_tables()
    row2 = lambda a: a.reshape(L, 1, -1)

    rw_s0 = jnp.concatenate([jnp.zeros((BATCH, L, 2, A_HEADS, A_HEAD, A_HEAD), F32),
                             jnp.swapaxes(state_rwkv, -1, -2)], axis=0)
    m_col = jnp.broadcast_to(state_mlstm_m[..., None, None], state_mlstm_m.shape + (B_DK, 1))
    ml_dec = jnp.concatenate([state_mlstm_c, state_mlstm_n[..., None], m_col,
                              jnp.zeros(state_mlstm_m.shape + (B_DK, 128 - B_DK - 2), F32)], axis=-1)
    ml_s0 = jnp.concatenate([jnp.zeros((BATCH,) + ml_dec.shape[1:], F32), ml_dec], axis=0)
    kpe_cache = jnp.pad(cache_mla_kpe, ((0, 0), (0, 0), (0, 0), (C_NOPE, HEAD_G - C_NOPE - C_ROPE)))

    mod = _modulation(cond, ada_w, ada_b)
    n_prompt = BATCH * SEQ
    outs = {k: [] for k in ('rw', 'mlc', 'ckv', 'kpe')}
    for l in range(L):
        modt = mod[l][_MOD_ROW_OF_TILE].reshape(N_TILES, 1, 6 * D_MODEL)
        proj = _in_projection(x, modt, w_proj[l])
        rw, rw_fin = _rwkv(proj, rw_s0[:, l], row2(rw_mu)[l], rw_w0[l][:, None], rw_w2[l], rw_a0[l][:, None],
                           rw_a2[l], rw_g2[l], row2(rw_kk)[l], row2(rw_ka)[l], row2(rw_rk)[l])
        ml, ml_fin = _mlstm(proj, ml_s0[:, l], ml_conv[l], gate_b[l])
        q, ckv, kpe = _mla_pre(proj, row2(mla_q_norm)[l], row2(mla_kv_norm)[l], wuq[l], rope_tabs)
        ckv_all = jnp.concatenate([ckv[:n_prompt]] + [
            t for s in range(DEC_BATCH)
            for t in (cache_mla_ckv[s, l], ckv[n_prompt + s * DEC_SEQ:n_prompt + (s + 1) * DEC_SEQ])], axis=0)
        kpe_all = jnp.concatenate([kpe[:n_prompt]] + [
            t for s in range(DEC_BATCH)
            for t in (kpe_cache[s, l], kpe[n_prompt + s * DEC_SEQ:n_prompt + (s + 1) * DEC_SEQ])], axis=0)
        kf, vf = _mla_kv(ckv_all, kpe_all, wuk[l], wuv[l])
        lk = PAST_LEN + DEC_SEQ
        yc_p = _attention(q[:n_prompt], kf[:n_prompt].reshape(BATCH, SEQ, -1),
                          vf[:n_prompt].reshape(BATCH, SEQ, -1), SEQ)
        yc_s = _attention(q[n_prompt:], kf[n_prompt:].reshape(DEC_BATCH, lk, -1),
                          vf[n_prompt:].reshape(DEC_BATCH, lk, -1), 256)
        yc = jnp.concatenate([yc_p, yc_s], axis=0)
        x1, u2, logits = _merge(x, rw, ml, yc, proj, modt, row2(rw_ln_w)[l], row2(rw_ln_b)[l],
                                row2(ml_ln_w)[l], row2(ml_ln_b)[l], pa[l], pb[l], pc[l], wo[l],
                                row2(ln1_w)[l], row2(ln1_b)[l], rtw[l], rtb[l])
        tok_sorted, w_sorted, inv, items = _route(logits[:, :N_EXPERTS])
        y_sorted = _moe_experts(l, u2[tok_sorted], items, w1s, b1g, b1l, moe_w2, b2, w_sorted[:, None])
        y_slots = y_sorted[inv.reshape(N_TOK, TOP_K).T.reshape(-1)].reshape(TOP_K, N_TOK, D_MODEL)
        x = _final_norm(x1, y_slots, modt, row2(ln2_w)[l], row2(ln2_b)[l])
        outs['rw'].append(jnp.swapaxes(jnp.stack([rw_fin[0][:BATCH], rw_fin[1][:BATCH]], axis=1), -1, -2))
        outs['mlc'].append(ml_fin[:BATCH])
        outs['ckv'].append(ckv[:n_prompt].reshape(BATCH, SEQ, C_KV_LORA))
        outs['kpe'].append(kpe[:n_prompt, C_NOPE:C_NOPE + C_ROPE].reshape(BATCH, SEQ, C_ROPE))

    stack = lambda k: jnp.stack(outs[k], axis=1)
    mlc = stack('mlc')
    return (x[:n_prompt].reshape(BATCH, SEQ, D_MODEL), x[n_prompt:].reshape(DEC_BATCH, DEC_SEQ, D_MODEL),
            stack('rw'), mlc[..., 0:B_DK], mlc[..., N_COL], mlc[..., 0, M_COL], stack('ckv'), stack('kpe'))
```

```python
import functools

import numpy as np
import jax
import jax.numpy as jnp
from jax import lax
from jax.experimental import pallas as pl
from jax.experimental.pallas import tpu as pltpu

F32 = jnp.float32
BF16 = jnp.bfloat16

D_MODEL = 1024
BATCH = 16
SEQ = 256
DEPTH = 4
DEC_BATCH = 2
DEC_SEQ = 4096
PAST_LEN = 256
GRID_W = 64

A_HEADS = 4
A_HEAD = 64
A_W = 256
A_GN_EPS = 64e-5
B_HEADS = 4
B_DK = 64
B_W = 256
C_HEADS = 8
C_NOPE = 64
C_ROPE = 32
C_V = 64
C_Q_LORA = 256
C_KV_LORA = 128
C_W = 512
ROPE_BASE = 10000.0
N_EXPERTS = 32
TOP_K = 4
D_EXPERT = 1024
SWIGLU_LIMIT = 7.0
SWIGLU_ALPHA = 1.702
MOE_BLOCK = 256
A_IN = 1024
B_IN = 1040
C_IN = 416
DN_ALPHA = (2 * DEPTH) ** 0.25
LN_EPS = 1e-5
RMS_EPS = 1e-6

TILE = 256
CHUNK = 64
N_TOK = BATCH * SEQ + DEC_BATCH * DEC_SEQ
N_TILES = N_TOK // TILE
N_PROMPT_TILES = BATCH * SEQ // TILE
TILES_PER_SAMPLE = DEC_SEQ // TILE
N_SEQ = BATCH + DEC_BATCH
HALO = 8

COL_A = 0
COL_G = 1024
COL_B = 4096
COL_C = 5120
COL_BG = 5632
N_PROJ = 5888

VMEM_LIMIT = 56 * 1024 * 1024


def _cparams(sem):
    return pltpu.CompilerParams(dimension_semantics=sem, vmem_limit_bytes=VMEM_LIMIT)


NN = ((1,), (0,))
NT = ((1,), (1,))
TN = ((0,), (0,))


def _dot(a, b, dims=NN):
    return lax.dot_general(a, b, (dims, ((), ())), preferred_element_type=F32)


def _split(x, n):
    parts, r = [], x
    for i in range(n):
        p = r.astype(BF16)
        parts.append(p)
        if i + 1 < n:
            r = r - p.astype(F32)
    return parts


def _mm(a, b, passes=1, dims=NN):
    if passes == 1:
        return _dot(a.astype(BF16), b.astype(BF16), dims)
    ah, al = _split(a, 2)
    bh, bl = _split(b, 2)
    return _dot(ah, bh, dims) + (_dot(al, bh, dims) + _dot(ah, bl, dims))


def _mm_exact_l(t01, x):
    x0, x1, x2 = _split(x, 3)
    return _dot(t01, x0) + (_dot(t01, x1) + _dot(t01, x2))


def _mm_exact_r(x, t01):
    x0, x1, x2 = _split(x, 3)
    return _dot(x0, t01) + (_dot(x1, t01) + _dot(x2, t01))


def _sigmoid(x):
    return 1.0 / (1.0 + jnp.exp(-x))


def _softplus(x):
    return jnp.maximum(x, 0.0) + jnp.log(1.0 + jnp.exp(-jnp.abs(x)))


def _block_ones(n, blk):
    r = lax.broadcasted_iota(jnp.int32, (n, n), 0) // blk
    c = lax.broadcasted_iota(jnp.int32, (n, n), 1) // blk
    return jnp.where(r == c, 1.0, 0.0).astype(BF16)


def _layer_norm(x, w, b):
    mu = jnp.mean(x, axis=-1, keepdims=True)
    xc = x - mu
    var = jnp.mean(xc * xc, axis=-1, keepdims=True)
    return xc * lax.rsqrt(var + LN_EPS) * w + b


def _head_norm(x, ones_blk, w, b, eps):
    mu = _mm_exact_r(x, ones_blk) * (1.0 / 64.0)
    xc = x - mu
    var = _mm_exact_r(xc * xc, ones_blk) * (1.0 / 64.0)
    return xc * lax.rsqrt(var + eps) * w + b


def _shifted(x, prev_row, next_row):
    rid = lax.broadcasted_iota(jnp.int32, x.shape, 0)
    xp = jnp.where(rid == 0, prev_row, pltpu.roll(x, 1, axis=0))
    xn = jnp.where(rid == x.shape[0] - 1, next_row, pltpu.roll(x, x.shape[0] - 1, axis=0))
    return xp, xn


def _schedule():
    seq_of_tile = np.concatenate([np.arange(BATCH), BATCH + np.repeat(np.arange(DEC_BATCH), TILES_PER_SAMPLE)])
    first_of_tile = np.ones(N_TILES, np.int32)
    last_of_tile = np.ones(N_TILES, np.int32)
    for s in range(DEC_BATCH):
        base = N_PROMPT_TILES + s * TILES_PER_SAMPLE
        first_of_tile[base + 1: base + TILES_PER_SAMPLE] = 0
        last_of_tile[base: base + TILES_PER_SAMPLE - 1] = 0
    tile_fwd = np.arange(N_TILES)
    tile_bwd = np.arange(N_TILES)
    for s in range(DEC_BATCH):
        base = N_PROMPT_TILES + s * TILES_PER_SAMPLE
        tile_bwd[base: base + TILES_PER_SAMPLE] = base + TILES_PER_SAMPLE - 1 - np.arange(TILES_PER_SAMPLE)
    tile = np.concatenate([tile_fwd, tile_bwd]).astype(np.int32)
    seq = seq_of_tile[tile].astype(np.int32)
    begins = np.concatenate([first_of_tile[tile_fwd], last_of_tile[tile_bwd]]).astype(np.int32)
    ends = np.concatenate([last_of_tile[tile_fwd], first_of_tile[tile_bwd]]).astype(np.int32)
    has_prev = (1 - first_of_tile).astype(np.int32)
    has_next = (1 - last_of_tile).astype(np.int32)
    return tile, seq, begins, ends, has_prev, has_next


_SCHED = _schedule()
_SEQ_OF_TILE = np.concatenate([np.arange(BATCH), BATCH + np.repeat(np.arange(DEC_BATCH), TILES_PER_SAMPLE)]).astype(np.int32)
_MOD_ROW_OF_TILE = np.concatenate([np.zeros(N_PROMPT_TILES), 1 + np.repeat(np.arange(DEC_BATCH), TILES_PER_SAMPLE)]).astype(np.int32)


def _direction_masks(d):
    row = lax.broadcasted_iota(jnp.int32, (CHUNK, CHUNK), 0)
    col = lax.broadcasted_iota(jnp.int32, (CHUNK, CHUNK), 1)
    diff = (row - col) * (1 - 2 * d)
    return diff >= 0, diff > 0, diff <= 0


def _mod_kernel(c_ref, w_ref, b_ref, o_ref):
    c = c_ref[...]
    o_ref[0] = _mm(c * _sigmoid(c), w_ref[0], 3) + b_ref[0]


def _modulation(cond, ada_w, ada_b):
    tn = 1536
    return pl.pallas_call(
        _mod_kernel,
        grid=(DEPTH, 6 * D_MODEL // tn),
        in_specs=[pl.BlockSpec((8, D_MODEL), lambda l, j: (0, 0)),
                  pl.BlockSpec((1, D_MODEL, tn), lambda l, j: (l, 0, j)),
                  pl.BlockSpec((1, 1, tn), lambda l, j: (l, 0, j))],
        out_specs=pl.BlockSpec((1, 8, tn), lambda l, j: (l, 0, j)),
        out_shape=jax.ShapeDtypeStruct((DEPTH, 8, 6 * D_MODEL), F32),
        compiler_params=_cparams(("parallel", "parallel")),
        name="ada_modulation",
    )(cond, ada_w, ada_b.reshape(DEPTH, 1, 6 * D_MODEL))


IN_TM = 512
IN_TN = N_PROJ // 2


def _inproj_kernel(x_ref, sh_ref, sc_ref, w_ref, o_ref):
    u = x_ref[...] * (1.0 + sc_ref[0]) + sh_ref[0]
    o_ref[...] = _dot(u.astype(BF16), w_ref[...])


def _in_projection(x, modt, w):
    rep = IN_TM // TILE
    return pl.pallas_call(
        _inproj_kernel,
        grid=(N_PROJ // IN_TN, N_TOK // IN_TM),
        in_specs=[pl.BlockSpec((IN_TM, D_MODEL), lambda j, i: (i, 0)),
                  pl.BlockSpec((1, 1, D_MODEL), lambda j, i: (rep * i, 0, 0)),
                  pl.BlockSpec((1, 1, D_MODEL), lambda j, i: (rep * i, 0, 1)),
                  pl.BlockSpec((D_MODEL, IN_TN), lambda j, i: (0, j))],
        out_specs=pl.BlockSpec((IN_TM, IN_TN), lambda j, i: (i, j)),
        out_shape=jax.ShapeDtypeStruct((N_TOK, N_PROJ), F32),
        compiler_params=_cparams(("parallel", "parallel")),
        name="in_projection",
    )(x, modt, modt, w)


RW_P = 1
RW_GRAM_P = 1
RW_UPD_P = 1
RW_INV_P = 1
RW_ST_P = 1
HB = A_HEADS * CHUNK


def _pre(x, passes):
    return tuple(_split(x, 2)) if passes == 3 else (x.astype(BF16),)


def _mmp(a, b, dims=NN):
    out = _dot(a[0], b[0], dims)
    if len(a) == 2 and len(b) == 2:
        out = out + (_dot(a[1], b[0], dims) + _dot(a[0], b[1], dims))
    return out


def _head_expand(m, same_head):
    return jnp.where(same_head, jnp.concatenate([m] * A_HEADS, axis=0), 0.0)


def _head_stack(m):
    return jnp.concatenate([m[:, h * A_HEAD:(h + 1) * A_HEAD] for h in range(A_HEADS)], axis=0)


def _head_unstack(m):
    return jnp.concatenate([m[h * CHUNK:(h + 1) * CHUNK, :] for h in range(A_HEADS)], axis=1)


def _rwkv_kernel(d, tile_s, seq_s, begin_s, end_s, hasprev_s, hasnext_s,
                 xa_ref, xp_ref, xn_ref, s0_ref, mu_ref, w0_ref, w2_ref, a0_ref, a2_ref, g2_ref,
                 kk_ref, ka_ref, rk_ref,
                 out_ref, sfin_ref,
                 st_scr):
    step = pl.program_id(0)
    tile = tile_s[step]

    @pl.when(begin_s[step] == 1)
    def _():
        st_scr[...] = s0_ref[0]

    x = xa_ref[...]
    prev_row = xp_ref[HALO - 1:HALO, :] * hasprev_s[tile].astype(F32)
    next_row = xn_ref[0:1, :] * hasnext_s[tile].astype(F32)
    xp, xn = _shifted(x, prev_row, next_row)
    xs = x + mu_ref[...] * (0.5 * (xp + xn) - x)
    r = xs[:, 0:256]
    k = xs[:, 256:512]
    v = xs[:, 512:768]
    wl = xs[:, 768:832]
    al = xs[:, 832:896]
    gl = xs[:, 896:1024]

    ones_blk = _block_ones(A_W, A_HEAD)
    kkv = k * kk_ref[...]
    nrm = jnp.sqrt(_mm_exact_r(kkv * kkv, ones_blk))
    kap = kkv / jnp.maximum(nrm, 1e-6)
    wpre = w0_ref[...] + _mm(jnp.tanh(wl), w2_ref[...], 3)
    lw = -jnp.exp(-_softplus(-wpre) - 0.5)
    a = _sigmoid(a0_ref[...] + _mm(al, a2_ref[...], 3))
    kd = k * (1.0 + (a - 1.0) * ka_ref[...])
    bonus = _mm_exact_r(r * kd * rk_ref[...], ones_blk) * v
    out_ref[:, 256:512] = bonus
    out_ref[:, 512:768] = _mm(_sigmoid(gl), g2_ref[...], 3)

    b = kap * a

    incl64, _, _ = _direction_masks(d)
    tinc = jnp.where(incl64, 1.0, 0.0).astype(BF16)
    row = lax.broadcasted_iota(jnp.int32, (HB, HB), 0)
    col = lax.broadcasted_iota(jnp.int32, (HB, HB), 1)
    same_head = (row // CHUNK) == (col // CHUNK)
    order = jnp.where(same_head, (row % CHUNK - col % CHUNK) * (1 - 2 * d), -1)
    incl = order >= 0
    strict = order > 0
    eye = jnp.where(row == col, 1.0, 0.0)
    xor = row ^ col
    level_masks = [(xor >> s) == 1 for s in range(CHUNK.bit_length() - 1)]
    n_chunks = TILE // CHUNK
    chunks = [c if d == 0 else n_chunks - 1 - c for c in range(n_chunks)]
    sl = [slice(c * CHUNK, (c + 1) * CHUNK) for c in chunks]
    rng = range(n_chunks)

    lwc = [lw[s] for s in sl]
    cs = [_mm_exact_l(tinc, x) for x in lwc]
    w_in = [jnp.exp(x) for x in cs]
    w_ex = [jnp.exp(cs[i] - lwc[i]) for i in rng]
    w_inv = [jnp.exp(-x) for x in cs]
    w_tot = [jnp.exp(jnp.sum(x, axis=0, keepdims=True)) for x in lwc]
    kt = [_head_expand(kap[sl[i]] * w_ex[i], same_head) for i in rng]
    rt = [_head_expand(r[sl[i]] * w_in[i], same_head) for i in rng]
    bt = [b[sl[i]] * w_inv[i] for i in rng]
    kdt = [kd[sl[i]] * w_inv[i] for i in rng]
    v_stack = [_head_stack(v[s]) for s in sl]
    v_st = [_pre(x, RW_P) for x in v_stack]
    v_su = v_st if RW_UPD_P == RW_P else [_pre(x, RW_UPD_P) for x in v_stack]
    gram = [_mm(jnp.concatenate([kt[i], rt[i]], axis=0),
                jnp.concatenate([bt[i]] * A_HEADS + [kdt[i]] * A_HEADS, axis=0), RW_GRAM_P, NT) for i in rng]
    l_b = [jnp.where(strict, g[:HB, :HB], 0.0) for g in gram]
    l_k = [jnp.where(strict, g[:HB, HB:], 0.0) for g in gram]
    m_b = [_pre(jnp.where(incl, g[HB:, :HB], 0.0), RW_P) for g in gram]
    m_k = [jnp.where(incl, g[HB:, HB:], 0.0) for g in gram]
    t_inv = [eye - jnp.where(level_masks[0], x, 0.0) for x in l_b]
    for mask in level_masks[1:]:
        tp = [_pre(t, RW_INV_P) for t in t_inv]
        w1 = [_mmp(tp[i], _pre(jnp.where(mask, l_b[i], 0.0), RW_INV_P)) for i in rng]
        t_inv = [t_inv[i] - _mmp(_pre(w1[i], RW_INV_P), tp[i]) for i in rng]
    lkv = [_mmp(_pre(l_k[i], RW_P), v_st[i]) for i in rng]
    xx = [_mm(t_inv[i], jnp.concatenate([kt[i], lkv[i]], axis=1), RW_P) for i in rng]
    xxp = [_pre(x, RW_P) for x in xx]
    mx = [_mmp(m_b[i], xxp[i]) for i in rng]
    rhat = [_pre(rt[i] - mx[i][:, :HB], RW_ST_P) for i in rng]
    y0 = [_mmp(_pre(m_k[i], RW_P), v_st[i]) - mx[i][:, HB:] for i in rng]
    xxu = xxp if RW_UPD_P == RW_P else [_pre(x, RW_UPD_P) for x in xx]
    bhp = [_pre(_head_expand(bt[i] * w_tot[i], same_head), RW_UPD_P) for i in rng]
    bx = [_mmp(bhp[i], xxu[i], TN) for i in rng]
    g_mat = [_pre(eye * w_tot[i] - bx[i][:, :HB], RW_ST_P) for i in rng]
    h_mat = [_mmp(_pre(_head_expand(kdt[i] * w_tot[i], same_head), RW_UPD_P), v_su[i], TN) - bx[i][:, HB:]
             for i in rng]

    st = st_scr[...]
    for i in rng:
        stp = _pre(st, RW_ST_P)
        out_ref[sl[i], 0:256] = _head_unstack(_mmp(rhat[i], stp) + y0[i])
        st = _mmp(g_mat[i], stp) + h_mat[i]
    st_scr[...] = st

    @pl.when(end_s[step] == 1)
    def _():
        sfin_ref[0] = st


def _rwkv_direction(d, proj, s0, mu, w0, w2, a0, a2, g2, kk, ka, rk):
    nrb = N_TOK // HALO
    per = TILE // HALO
    tile, seq, begins, ends, has_prev, has_next = _SCHED
    half = slice(d * N_TILES, (d + 1) * N_TILES)
    sched = tuple(jnp.asarray(a) for a in (tile[half], seq[half], begins[half], ends[half], has_prev, has_next))

    def const(shape):
        return pl.BlockSpec(shape, lambda i, *_: (0,) * len(shape))

    grid_spec = pltpu.PrefetchScalarGridSpec(
        num_scalar_prefetch=6,
        grid=(N_TILES,),
        in_specs=[
            pl.BlockSpec((TILE, A_IN), lambda i, t, *_: (t[i], COL_A // A_IN)),
            pl.BlockSpec((HALO, A_IN), lambda i, t, *_: (jnp.maximum(t[i] * per - 1, 0), 0)),
            pl.BlockSpec((HALO, A_IN), lambda i, t, *_: (jnp.minimum(t[i] * per + per, nrb - 1), 0)),
            pl.BlockSpec((1, HB, A_HEAD), lambda i, t, s, *_: (s[i], 0, 0)),
            const((1, A_IN)),
            const((1, A_W)), const((64, A_W)), const((1, A_W)), const((64, A_W)),
            const((128, A_W)), const((1, A_W)), const((1, A_W)), const((1, A_W)),
        ],
        out_specs=[
            pl.BlockSpec((TILE, 768), lambda i, t, *_: (t[i], 0)),
            pl.BlockSpec((1, HB, A_HEAD), lambda i, t, s, *_: (s[i], 0, 0)),
        ],
        scratch_shapes=[pltpu.VMEM((HB, A_HEAD), F32)],
    )
    return pl.pallas_call(
        functools.partial(_rwkv_kernel, d),
        grid_spec=grid_spec,
        out_shape=[jax.ShapeDtypeStruct((N_TOK, 768), F32),
                   jax.ShapeDtypeStruct((N_SEQ, HB, A_HEAD), F32)],
        compiler_params=_cparams(("arbitrary",)),
        name="rwkv7_mixer_fwd" if d == 0 else "rwkv7_mixer_bwd",
    )(*sched, proj, proj, proj, s0, mu, w0, w2, a0, a2, g2, kk, ka, rk)


def _rwkv(proj, s0, mu, w0, w2, a0, a2, g2, kk, ka, rk):
    outs = [_rwkv_direction(d, proj, s0[:, d].reshape(N_SEQ, HB, A_HEAD), mu, w0[d], w2[d], a0[d], a2[d],
                            g2, kk, ka, rk) for d in range(2)]
    fin = tuple(o[1].reshape(N_SEQ, A_HEADS, A_HEAD, A_HEAD) for o in outs)
    return (outs[0][0], outs[1][0]), fin


ML_P = 1
N_COL = B_DK
M_COL = B_DK + 1
NEG = -1e30


def _per_head_col(x, first):
    return jnp.concatenate([x[:, first + h:first + h + 1] for h in range(B_HEADS)], axis=0)


def _per_head_row(x_t, first):
    return jnp.concatenate([x_t[first + h:first + h + 1, :] for h in range(B_HEADS)], axis=1)


def _head_fill(x, first):
    return jnp.concatenate([jnp.broadcast_to(x[:, first + h:first + h + 1], (CHUNK, 1)) for h in range(B_HEADS)],
                           axis=0)


def _head_max(x):
    return jnp.concatenate(
        [jnp.broadcast_to(jnp.max(x[h * CHUNK:(h + 1) * CHUNK], axis=0, keepdims=True), (CHUNK, 1))
         for h in range(B_HEADS)], axis=0)


def _mlstm_kernel(d, tile_s, seq_s, begin_s, end_s, hasprev_s, hasnext_s,
                  xb_ref, xp_ref, xn_ref, gt_ref, s0_ref, cw_ref, gb_ref,
                  out_ref, sfin_ref,
                  cn_scr, m_scr):
    step = pl.program_id(0)
    tile = tile_s[step]

    @pl.when(begin_s[step] == 1)
    def _():
        s0 = s0_ref[0]
        cn_scr[...] = s0
        m_scr[...] = s0[:, M_COL:M_COL + 1]

    qk = xb_ref[:, 0:512]
    prev_row = xp_ref[HALO - 1:HALO, 0:512] * hasprev_s[tile].astype(F32)
    next_row = xn_ref[0:1, 0:512] * hasnext_s[tile].astype(F32)
    qp, qn = _shifted(qk, prev_row, next_row)
    conv = cw_ref[0:1, :] * qp + cw_ref[1:2, :] * qk + cw_ref[2:3, :] * qn
    act = conv * _sigmoid(conv)
    q = act[:, 0:256]
    k = act[:, 256:512] * (B_DK ** -0.5)
    v = xb_ref[:, 512:768]
    li = gt_ref[...] + gb_ref[...]
    lf = -_softplus(-li)

    incl64, _, incl64_t = _direction_masks(d)
    tinc = jnp.where(incl64, 1.0, 0.0).astype(BF16)
    tinc_t = jnp.where(incl64_t, 1.0, 0.0).astype(BF16)
    row = lax.broadcasted_iota(jnp.int32, (HB, HB), 0)
    col = lax.broadcasted_iota(jnp.int32, (HB, HB), 1)
    same_head = (row // CHUNK) == (col // CHUNK)
    causal = jnp.where(same_head, (row % CHUNK - col % CHUNK) * (1 - 2 * d), -1) >= 0
    lane = lax.broadcasted_iota(jnp.int32, (HB, 128), 1)
    n_chunks = TILE // CHUNK
    chunks = [c if d == 0 else n_chunks - 1 - c for c in range(n_chunks)]
    sl = [slice(c * CHUNK, (c + 1) * CHUNK) for c in chunks]
    rng = range(n_chunks)
    fg = B_HEADS

    li_c = [li[s] for s in sl]
    lf_c = [lf[s] for s in sl]
    b_c = [_mm_exact_l(tinc, x) for x in lf_c]
    b_r = [_mm_exact_r(x.T, tinc_t) for x in lf_c]
    li_r = [x.T for x in li_c]
    bc = [_per_head_col(x, fg) for x in b_c]
    lic = [_per_head_col(x, 0) for x in li_c]
    br = [_per_head_row(x, fg) for x in b_r]
    lir = [_per_head_row(x, 0) for x in li_r]
    be = [_head_fill(jnp.sum(x, axis=0, keepdims=True), fg) for x in lf_c]
    wlog = [be[i] - bc[i] + lic[i] for i in rng]
    wmax = [_head_max(x) for x in wlog]
    m_old, m_new = [], []
    m = m_scr[...]
    for i in rng:
        m_old.append(m)
        m = jnp.maximum(be[i] + m, wmax[i])
        m_new.append(m)
    m_scr[...] = m

    dlog = [jnp.where(causal, bc[i] - br[i] + lir[i], NEG) for i in rng]
    inter = [bc[i] + m_old[i] for i in rng]
    mj = [jnp.maximum(jnp.max(dlog[i], axis=1, keepdims=True), inter[i]) for i in rng]
    q_exp = [_pre(_head_expand(q[s], same_head), ML_P) for s in sl]
    k_exp = [_head_expand(k[s], same_head) for s in sl]
    s_mat = [_mmp(q_exp[i], _pre(jnp.concatenate([k[sl[i]]] * B_HEADS, axis=0), ML_P), NT)
             * jnp.exp(dlog[i] - mj[i]) for i in rng]
    e_int = [jnp.exp(inter[i] - mj[i]) for i in rng]
    v_aug = [jnp.where(lane == N_COL, 1.0,
                       jnp.concatenate([_head_stack(v[s]), jnp.zeros((HB, 128 - B_DK), F32)], axis=1)) for s in sl]
    sv = [_mm(s_mat[i], v_aug[i], ML_P) for i in rng]
    wk = [jnp.exp(wlog[i] - m_new[i]) for i in rng]
    dec = [jnp.exp(be[i] + m_old[i] - m_new[i]) for i in rng]
    kv = [_mm(k_exp[i], wk[i] * v_aug[i], ML_P, TN) for i in rng]
    floor = [jnp.exp(-x) for x in mj]

    cn = cn_scr[...]
    for i in rng:
        nd = sv[i] + e_int[i] * _mmp(q_exp[i], _pre(cn, ML_P))
        den = jnp.maximum(jnp.abs(nd[:, N_COL:N_COL + 1]), floor[i])
        out_ref[sl[i], :] = _head_unstack(nd[:, 0:B_DK] / den)
        cn = dec[i] * cn + kv[i]
    cn_scr[...] = cn

    @pl.when(end_s[step] == 1)
    def _():
        sfin_ref[0] = jnp.where(lane == M_COL, m, cn)


def _mlstm_direction(d, proj, s0, conv_w, gate_b):
    nrb = N_TOK // HALO
    per = TILE // HALO
    tile, seq, begins, ends, has_prev, has_next = _SCHED
    half = slice(d * N_TILES, (d + 1) * N_TILES)
    sched = tuple(jnp.asarray(a) for a in (tile[half], seq[half], begins[half], ends[half], has_prev, has_next))
    cb = COL_B // 1024
    grid_spec = pltpu.PrefetchScalarGridSpec(
        num_scalar_prefetch=6,
        grid=(N_TILES,),
        in_specs=[
            pl.BlockSpec((TILE, 1024), lambda i, t, *_: (t[i], cb)),
            pl.BlockSpec((HALO, 1024), lambda i, t, *_: (jnp.maximum(t[i] * per - 1, 0), cb)),
            pl.BlockSpec((HALO, 1024), lambda i, t, *_: (jnp.minimum(t[i] * per + per, nrb - 1), cb)),
            pl.BlockSpec((TILE, 128), lambda i, t, *_: (t[i], COL_BG // 128 + d)),
            pl.BlockSpec((1, HB, 128), lambda i, t, s, *_: (s[i], 0, 0)),
            pl.BlockSpec((3, 512), lambda i, *_: (0, 0)),
            pl.BlockSpec((1, 128), lambda i, *_: (0, 0)),
        ],
        out_specs=[
            pl.BlockSpec((TILE, B_W), lambda i, t, *_: (t[i], 0)),
            pl.BlockSpec((1, HB, 128), lambda i, t, s, *_: (s[i], 0, 0)),
        ],
        scratch_shapes=[pltpu.VMEM((HB, 128), F32), pltpu.VMEM((HB, 1), F32)],
    )
    return pl.pallas_call(
        functools.partial(_mlstm_kernel, d),
        grid_spec=grid_spec,
        out_shape=[jax.ShapeDtypeStruct((N_TOK, B_W), F32),
                   jax.ShapeDtypeStruct((N_SEQ, HB, 128), F32)],
        compiler_params=_cparams(("arbitrary",)),
        name="mlstm_mixer_fwd" if d == 0 else "mlstm_mixer_bwd",
    )(*sched, proj, proj, proj, proj, s0, conv_w, gate_b)


def _mlstm(proj, s0, conv_w, gate_b):
    outs = [_mlstm_direction(d, proj, s0[:, d].reshape(N_SEQ, HB, 128), conv_w, gate_b[d]) for d in range(2)]
    fin = jnp.stack([o[1].reshape(N_SEQ, B_HEADS, B_DK, 128) for o in outs], axis=1)
    return (outs[0][0], outs[1][0]), fin


HEAD_G = 128
ATT_SCALE = (C_NOPE + C_ROPE) ** -0.5


def _rope(x, cos, sin_lo, sin_hi):
    return x * cos + pltpu.roll(x, 16, axis=1) * sin_hi + pltpu.roll(x, HEAD_G - 16, axis=1) * sin_lo


def _mla_pre_kernel(xc_ref, qn_ref, kvn_ref, wuq_ref, cos_ref, slo_ref, shi_ref, q_ref, ckv_ref, kpe_ref):
    q_dn = xc_ref[:, 0:C_Q_LORA]
    qn = q_dn * lax.rsqrt(jnp.mean(q_dn * q_dn, axis=-1, keepdims=True) + RMS_EPS) * qn_ref[...]
    q = _dot(qn.astype(BF16), wuq_ref[...])
    cos, slo, shi = cos_ref[...], slo_ref[...], shi_ref[...]
    for h in range(C_HEADS):
        sl = slice(h * HEAD_G, (h + 1) * HEAD_G)
        q_ref[:, sl] = (_rope(q[:, sl], cos, slo, shi) * ATT_SCALE).astype(BF16)
    kv_dn = xc_ref[:, C_Q_LORA:C_Q_LORA + C_KV_LORA]
    ckv_ref[...] = kv_dn * lax.rsqrt(jnp.mean(kv_dn * kv_dn, axis=-1, keepdims=True) + RMS_EPS) * kvn_ref[...]
    kpe_ref[...] = _rope(xc_ref[:, 384:512], cos, slo, shi)


def _mla_pre(proj, q_norm, kv_norm, wuq, rope_tabs):
    cc = COL_C // 512

    def tab_idx(i):
        return (jnp.where(i < N_PROMPT_TILES, TILES_PER_SAMPLE, (i - N_PROMPT_TILES) % TILES_PER_SAMPLE), 0)

    tab_spec = pl.BlockSpec((TILE, HEAD_G), tab_idx)
    return pl.pallas_call(
        _mla_pre_kernel,
        grid=(N_TILES,),
        in_specs=[pl.BlockSpec((TILE, 512), lambda i: (i, cc)),
                  pl.BlockSpec((1, C_Q_LORA), lambda i: (0, 0)),
                  pl.BlockSpec((1, C_KV_LORA), lambda i: (0, 0)),
                  pl.BlockSpec((C_Q_LORA, C_HEADS * HEAD_G), lambda i: (0, 0)),
                  tab_spec, tab_spec, tab_spec],
        out_specs=[pl.BlockSpec((TILE, C_HEADS * HEAD_G), lambda i: (i, 0)),
                   pl.BlockSpec((TILE, C_KV_LORA), lambda i: (i, 0)),
                   pl.BlockSpec((TILE, HEAD_G), lambda i: (i, 0))],
        out_shape=[jax.ShapeDtypeStruct((N_TOK, C_HEADS * HEAD_G), BF16),
                   jax.ShapeDtypeStruct((N_TOK, C_KV_LORA), F32),
                   jax.ShapeDtypeStruct((N_TOK, HEAD_G), F32)],
        compiler_params=_cparams(("parallel",)),
        name="mla_pre",
    )(proj, q_norm, kv_norm, wuq, *rope_tabs)


def _mla_kv_kernel(ckv_ref, kpe_ref, wuk_ref, wuv_ref, k_ref, v_ref):
    ckv = ckv_ref[...].astype(BF16)
    kn = _dot(ckv, wuk_ref[...])
    kpe = kpe_ref[...]
    for h in range(C_HEADS):
        sl = slice(h * HEAD_G, (h + 1) * HEAD_G)
        k_ref[:, sl] = (kn[:, sl] + kpe).astype(BF16)
    v_ref[...] = _dot(ckv, wuv_ref[...]).astype(BF16)


def _mla_kv(ckv_all, kpe_all, wuk, wuv):
    n = ckv_all.shape[0]
    return pl.pallas_call(
        _mla_kv_kernel,
        grid=(n // TILE,),
        in_specs=[pl.BlockSpec((TILE, C_KV_LORA), lambda i: (i, 0)),
                  pl.BlockSpec((TILE, HEAD_G), lambda i: (i, 0)),
                  pl.BlockSpec((C_KV_LORA, C_HEADS * HEAD_G), lambda i: (0, 0)),
                  pl.BlockSpec((C_KV_LORA, C_W), lambda i: (0, 0))],
        out_specs=[pl.BlockSpec((TILE, C_HEADS * HEAD_G), lambda i: (i, 0)),
                   pl.BlockSpec((TILE, C_W), lambda i: (i, 0))],
        out_shape=[jax.ShapeDtypeStruct((n, C_HEADS * HEAD_G), BF16),
                   jax.ShapeDtypeStruct((n, C_W), BF16)],
        compiler_params=_cparams(("parallel",)),
        name="mla_kv",
    )(ckv_all, kpe_all, wuk, wuv)


def _attn_kernel(q_ref, k_ref, v_ref, o_ref):
    for h in range(C_HEADS):
        sl = slice(h * HEAD_G, (h + 1) * HEAD_G)
        s = _dot(q_ref[:, sl], k_ref[0, :, sl], NT)
        e = jnp.exp(s - jnp.max(s, axis=1, keepdims=True))
        den = jnp.sum(e, axis=1, keepdims=True)
        o = _dot(e.astype(BF16), v_ref[0, :, h * C_V:(h + 1) * C_V])
        o_ref[:, h * C_V:(h + 1) * C_V] = o / den


def _attention(q, k, v, tq):
    n_seq, lk, _ = k.shape
    lq = q.shape[0] // n_seq
    nqb = lq // tq
    return pl.pallas_call(
        _attn_kernel,
        grid=(n_seq, nqb),
        in_specs=[pl.BlockSpec((tq, C_HEADS * HEAD_G), lambda s, j: (s * nqb + j, 0)),
                  pl.BlockSpec((1, lk, C_HEADS * HEAD_G), lambda s, j: (s, 0, 0)),
                  pl.BlockSpec((1, lk, C_W), lambda s, j: (s, 0, 0))],
        out_specs=pl.BlockSpec((tq, C_W), lambda s, j: (s * nqb + j, 0)),
        out_shape=jax.ShapeDtypeStruct((q.shape[0], C_W), F32),
        compiler_params=_cparams(("parallel", "parallel")),
        name="mla_attention",
    )(q, k, v)


def _merge_kernel(x_ref, rw0_ref, rw1_ref, ml0_ref, ml1_ref, yc_ref, ga_ref, gb_ref, gc_ref, og_ref,
                  g1_ref, sh2_ref, sc2_ref, rwlw_ref, rwlb_ref, mllw_ref, mllb_ref,
                  pa_ref, pb_ref, pc_ref, wo_ref, l1w_ref, l1b_ref, rtw_ref, rtb_ref,
                  x1_ref, u2_ref, lg_ref):
    ones_blk = _block_ones(A_W, A_HEAD)
    rw0 = rw0_ref[...]
    rw1 = rw1_ref[...]
    ya = _head_norm(rw0[:, 0:256] + rw1[:, 0:256], ones_blk, rwlw_ref[...], rwlb_ref[...], A_GN_EPS)
    ya = (ya + rw0[:, 256:512] + rw1[:, 256:512]) * rw0[:, 512:768]
    yb = _head_norm(ml0_ref[...] + ml1_ref[...], ones_blk, mllw_ref[...], mllb_ref[...], LN_EPS)
    yb = yb * _sigmoid(og_ref[...])
    merged = (_sigmoid(ga_ref[...]) * _dot(ya.astype(BF16), pa_ref[...])
              + _sigmoid(gb_ref[...]) * _dot(yb.astype(BF16), pb_ref[...])
              + _sigmoid(gc_ref[...]) * _dot(yc_ref[...].astype(BF16), pc_ref[...]))
    mix = _dot(merged.astype(BF16), wo_ref[...])
    x1 = _layer_norm(DN_ALPHA * x_ref[...] + g1_ref[0] * mix, l1w_ref[...], l1b_ref[...])
    x1_ref[...] = x1
    u2 = x1 * (1.0 + sc2_ref[0]) + sh2_ref[0]
    u2_ref[...] = u2.astype(BF16)
    lg_ref[...] = _mm(u2, rtw_ref[...], 3) + rtb_ref[...]


def _merge(x, rw, ml, yc, proj, modt, rwlw, rwlb, mllw, mllb, pa, pb, pc, wo, l1w, l1b, rtw, rtb):
    gcol = COL_G // 1024

    def row(shape):
        return pl.BlockSpec(shape, lambda i: (0, 0))

    def mod(kk):
        return pl.BlockSpec((1, 1, D_MODEL), lambda i: (i, 0, kk))

    return pl.pallas_call(
        _merge_kernel,
        grid=(N_TILES,),
        in_specs=[pl.BlockSpec((TILE, D_MODEL), lambda i: (i, 0)),
                  pl.BlockSpec((TILE, 768), lambda i: (i, 0)),
                  pl.BlockSpec((TILE, 768), lambda i: (i, 0)),
                  pl.BlockSpec((TILE, B_W), lambda i: (i, 0)),
                  pl.BlockSpec((TILE, B_W), lambda i: (i, 0)),
                  pl.BlockSpec((TILE, C_W), lambda i: (i, 0)),
                  pl.BlockSpec((TILE, D_MODEL), lambda i: (i, gcol)),
                  pl.BlockSpec((TILE, D_MODEL), lambda i: (i, gcol + 1)),
                  pl.BlockSpec((TILE, D_MODEL), lambda i: (i, gcol + 2)),
                  pl.BlockSpec((TILE, B_W), lambda i: (i, (COL_B + 768) // B_W)),
                  mod(2), mod(3), mod(4),
                  row((1, A_W)), row((1, A_W)), row((1, B_W)), row((1, B_W)),
                  row((A_W, D_MODEL)), row((B_W, D_MODEL)), row((C_W, D_MODEL)), row((D_MODEL, D_MODEL)),
                  row((1, D_MODEL)), row((1, D_MODEL)), row((D_MODEL, 128)), row((1, 128))],
        out_specs=[pl.BlockSpec((TILE, D_MODEL), lambda i: (i, 0)),
                   pl.BlockSpec((TILE, D_MODEL), lambda i: (i, 0)),
                   pl.BlockSpec((TILE, 128), lambda i: (i, 0))],
        out_shape=[jax.ShapeDtypeStruct((N_TOK, D_MODEL), F32),
                   jax.ShapeDtypeStruct((N_TOK, D_MODEL), BF16),
                   jax.ShapeDtypeStruct((N_TOK, 128), F32)],
        compiler_params=_cparams(("parallel",)),
        name="merge_postnorm_router",
    )(x, rw[0], rw[1], ml[0], ml[1], yc, proj, proj, proj, proj, modt, modt, modt,
      rwlw, rwlb, mllw, mllb, pa, pb, pc, wo, l1w, l1b, rtw, rtb)


N_ASSIGN = N_TOK * TOP_K
N_ROW_BLOCKS = N_ASSIGN // MOE_BLOCK
N_ITEMS = N_ROW_BLOCKS + N_EXPERTS


def _moe_kernel(blk_s, exp_s, lo_s, hi_s, init_s, x_ref, w1g_ref, w1l_ref, b1g_ref, b1l_ref, w2_ref, b2_ref,
                rw_ref, y_ref):
    w = pl.program_id(0)

    @pl.when(init_s[w] == 1)
    def _():
        y_ref[...] = jnp.zeros_like(y_ref)

    @pl.when(hi_s[w] > lo_s[w])
    def _():
        x = x_ref[...]
        hg = jnp.minimum(_dot(x, w1g_ref[0, 0, 0]) + b1g_ref[0, 0], SWIGLU_LIMIT)
        hl = jnp.clip(_dot(x, w1l_ref[0, 0, 0]) + b1l_ref[0, 0], -SWIGLU_LIMIT, SWIGLU_LIMIT)
        act = hg * _sigmoid(SWIGLU_ALPHA * hg) * (hl + 1.0)
        y = _dot(act.astype(BF16), w2_ref[0, 0].astype(BF16)) + b2_ref[0, 0]
        rid = lax.broadcasted_iota(jnp.int32, (MOE_BLOCK, 1), 0)
        mine = (rid >= lo_s[w]) & (rid < hi_s[w])
        y_ref[...] = (y_ref[...].astype(F32) + jnp.where(mine, y * rw_ref[...], 0.0)).astype(y_ref.dtype)


def _moe_experts(l, x_sorted, items, w1s, b1g, b1l, w2, b2, w_sorted):
    def wspec(shape):
        return pl.BlockSpec((1, 1) + shape, lambda i, blk, ex, *_: (l, ex[i], 0, 0))

    def w1spec(half):
        return pl.BlockSpec((1, 1, 1, D_MODEL, D_EXPERT), lambda i, blk, ex, *_: (l, ex[i], half, 0, 0))

    grid_spec = pltpu.PrefetchScalarGridSpec(
        num_scalar_prefetch=5,
        grid=(N_ITEMS,),
        in_specs=[pl.BlockSpec((MOE_BLOCK, D_MODEL), lambda i, blk, *_: (blk[i], 0)),
                  w1spec(0), w1spec(1),
                  wspec((1, D_EXPERT)), wspec((1, D_EXPERT)),
                  wspec((D_EXPERT, D_MODEL)), wspec((1, D_MODEL)),
                  pl.BlockSpec((MOE_BLOCK, 1), lambda i, blk, *_: (blk[i], 0))],
        out_specs=pl.BlockSpec((MOE_BLOCK, D_MODEL), lambda i, blk, *_: (blk[i], 0)),
    )
    return pl.pallas_call(
        _moe_kernel,
        grid_spec=grid_spec,
        out_shape=jax.ShapeDtypeStruct((N_ASSIGN, D_MODEL), BF16),
        compiler_params=_cparams(("arbitrary",)),
        name="moe_experts",
    )(*items, x_sorted, w1s, w1s, b1g, b1l, w2, b2, w_sorted)


def _final_kernel(x_ref, f_ref, g2_ref, w_ref, b_ref, o_ref):
    ffn = (f_ref[0].astype(F32) + f_ref[1].astype(F32)) + (f_ref[2].astype(F32) + f_ref[3].astype(F32))
    o_ref[...] = _layer_norm(DN_ALPHA * x_ref[...] + g2_ref[0] * ffn, w_ref[...], b_ref[...])


def _final_norm(x1, y_slots, modt, w, b):
    return pl.pallas_call(
        _final_kernel,
        grid=(N_TILES,),
        in_specs=[pl.BlockSpec((TILE, D_MODEL), lambda i: (i, 0)),
                  pl.BlockSpec((TOP_K, TILE, D_MODEL), lambda i: (0, i, 0)),
                  pl.BlockSpec((1, 1, D_MODEL), lambda i: (i, 0, 5)),
                  pl.BlockSpec((1, D_MODEL), lambda i: (0, 0)),
                  pl.BlockSpec((1, D_MODEL), lambda i: (0, 0))],
        out_specs=pl.BlockSpec((TILE, D_MODEL), lambda i: (i, 0)),
        out_shape=jax.ShapeDtypeStruct((N_TOK, D_MODEL), F32),
        compiler_params=_cparams(("parallel",)),
        name="ffn_postnorm",
    )(x1, y_slots, modt, w, b)


def _route(logits):
    top_v, top_e = lax.top_k(logits, TOP_K)
    top_w = jax.nn.softmax(top_v, axis=-1)
    flat_e = top_e.reshape(-1).astype(jnp.int32)
    idx = jnp.arange(N_ASSIGN, dtype=jnp.int32)
    _, order, w_sorted, tok_sorted = lax.sort((flat_e, idx, top_w.reshape(-1), idx // TOP_K), num_keys=1)
    _, inv = lax.sort((order, idx), num_keys=1)
    ex = jnp.arange(N_EXPERTS, dtype=jnp.int32)
    counts = jnp.sum((flat_e[:, None] == ex[None, :]).astype(jnp.int32), axis=0)
    end = jnp.cumsum(counts)
    start = end - counts
    first_blk = start // MOE_BLOCK
    n_items = jnp.where(counts > 0, (end - 1) // MOE_BLOCK - first_blk + 1, 0)
    item_end = jnp.cumsum(n_items)
    item_start = item_end - n_items
    w = jnp.arange(N_ITEMS, dtype=jnp.int32)
    valid = w < item_end[-1]
    e_w = jnp.minimum(jnp.sum((item_end[None, :] <= w[:, None]).astype(jnp.int32), axis=1), N_EXPERTS - 1)
    pick = (e_w[:, None] == ex[None, :]).astype(jnp.int32)
    look = lambda tab: jnp.sum(pick * tab[None, :], axis=1)
    blk = jnp.where(valid, look(first_blk) + w - look(item_start), N_ROW_BLOCKS - 1)
    lo = jnp.where(valid, jnp.maximum(look(start) - blk * MOE_BLOCK, 0), 0)
    hi = jnp.where(valid, jnp.minimum(look(end) - blk * MOE_BLOCK, MOE_BLOCK), 0)
    e_last = jnp.max(jnp.where(counts > 0, ex, 0))
    e_w = jnp.where(valid, e_w, e_last)
    init = jnp.concatenate([jnp.ones((1,), jnp.int32), (blk[1:] != blk[:-1]).astype(jnp.int32)])
    items = tuple(a.astype(jnp.int32) for a in (blk, e_w, lo, hi, init))
    return tok_sorted, w_sorted, inv, items


def _rope_tables():
    rows = DEC_SEQ // GRID_W
    r, col = jnp.meshgrid(jnp.arange(rows, dtype=F32), jnp.arange(GRID_W, dtype=F32), indexing='ij')
    n_freq = C_ROPE // 4
    inv = 1.0 / (ROPE_BASE ** (jnp.arange(n_freq, dtype=F32) / n_freq))
    ang = jnp.concatenate([r.reshape(-1, 1) * inv, col.reshape(-1, 1) * inv], axis=-1)
    cos, sin = jnp.cos(ang), jnp.sin(ang)
    half = C_ROPE // 2
    one = jnp.ones((DEC_SEQ, C_NOPE), F32)
    zero = jnp.zeros((DEC_SEQ, C_NOPE), F32)
    tail1 = jnp.ones((DEC_SEQ, HEAD_G - C_NOPE - C_ROPE), F32)
    tail0 = jnp.zeros((DEC_SEQ, HEAD_G - C_NOPE - C_ROPE), F32)
    zh = jnp.zeros((DEC_SEQ, half), F32)
    t_cos = jnp.concatenate([one, cos, cos, tail1], axis=1)
    t_lo = jnp.concatenate([zero, -sin, zh, tail0], axis=1)
    t_hi = jnp.concatenate([zero, zh, sin, tail0], axis=1)
    ident = jnp.ones((TILE, HEAD_G), F32)
    nil = jnp.zeros((TILE, HEAD_G), F32)
    return (jnp.concatenate([t_cos, ident]), jnp.concatenate([t_lo, nil]), jnp.concatenate([t_hi, nil]))


def _pad_heads(w, n_heads, width):
    lead = w.shape[:-1]
    w = w.reshape(lead + (n_heads, width))
    w = jnp.pad(w, [(0, 0)] * len(lead) + [(0, 0), (0, HEAD_G - width)])
    return w.reshape(lead + (n_heads * HEAD_G,))


def kernel(x_prompt, x_sample, state_rwkv, state_mlstm_c, state_mlstm_n, state_mlstm_m, cache_mla_ckv,
           cache_mla_kpe, c, c_ctx, ada_w, ada_b, w_in, rw_mu, rw_w0, rw_w2, rw_a0, rw_a2, rw_g2, rw_kk,
           rw_ka, rw_rk, rw_ln_w, rw_ln_b, ml_conv, ml_gate_b, ml_ln_w, ml_ln_b, mla_q_norm, mla_wuq,
           mla_kv_norm, mla_wuk, mla_wuv, proj_a, proj_b, proj_c, w_out, ln1_w, ln1_b, router_w, router_b,
           moe_w1, moe_b1, moe_w2, moe_b2, ln2_w, ln2_b):
    L = DEPTH
    x = jnp.concatenate([x_prompt.reshape(-1, D_MODEL), x_sample.reshape(-1, D_MODEL)], axis=0)

    cond = jnp.concatenate([c_ctx[None], c, jnp.zeros((8 - 1 - DEC_BATCH, D_MODEL), F32)], axis=0)
    wa = w_in[:, :, 0:A_IN]
    wb = w_in[:, :, A_IN:A_IN + 1024]
    wbg = w_in[:, :, A_IN + 1024:A_IN + B_IN]
    wc = w_in[:, :, A_IN + B_IN:A_IN + B_IN + C_IN]
    wg = w_in[:, :, A_IN + B_IN + C_IN:]
    z = lambda n: jnp.zeros((L, D_MODEL, n), F32)
    w_proj = jnp.concatenate([
        wa, wg, wb,
        wc[:, :, 0:384], z(64), wc[:, :, 384:416], z(32),
        wbg[:, :, 0:8], z(120), wbg[:, :, 8:16], z(120)], axis=2).astype(BF16)
    gate_b = jnp.pad(ml_gate_b.reshape(L, 2, 1, 2 * B_HEADS), ((0, 0), (0, 0), (0, 0), (0, 128 - 2 * B_HEADS)))
    wuq = _pad_heads(mla_wuq, C_HEADS, C_NOPE + C_ROPE).astype(BF16)
    wuk = _pad_heads(mla_wuk, C_HEADS, C_NOPE).astype(BF16)
    wuv = mla_wuv.astype(BF16)
    pa, pb, pc, wo = proj_a.astype(BF16), proj_b.astype(BF16), proj_c.astype(BF16), w_out.astype(BF16)
    rtw = jnp.pad(router_w, ((0, 0), (0, 0), (0, 128 - N_EXPERTS)))
    rtb = jnp.pad(router_b, ((0, 0), (0, 128 - N_EXPERTS))).reshape(L, 1, 128)
    w1 = moe_w1.reshape(L, N_EXPERTS, D_MODEL, D_EXPERT, 2)
    w1s = jnp.moveaxis(w1, -1, 2).astype(BF16)
    b1 = moe_b1.reshape(L, N_EXPERTS, 1, D_EXPERT, 2)
    b1g, b1l = b1[..., 0], b1[..., 1]
    b2 = moe_b2.reshape(L, N_EXPERTS, 1, D_MODEL)
    rope_tabs = _rope_tables()
    row2 = lambda a: a.reshape(L, 1, -1)

    rw_s0 = jnp.concatenate([jnp.zeros((BATCH, L, 2, A_HEADS, A_HEAD, A_HEAD), F32),
                             jnp.swapaxes(state_rwkv, -1, -2)], axis=0)
    m_col = jnp.broadcast_to(state_mlstm_m[..., None, None], state_mlstm_m.shape + (B_DK, 1))
    ml_dec = jnp.concatenate([state_mlstm_c, state_mlstm_n[..., None], m_col,
                              jnp.zeros(state_mlstm_m.shape + (B_DK, 128 - B_DK - 2), F32)], axis=-1)
    ml_s0 = jnp.concatenate([jnp.zeros((BATCH,) + ml_dec.shape[1:], F32), ml_dec], axis=0)
    kpe_cache = jnp.pad(cache_mla_kpe, ((0, 0), (0, 0), (0, 0), (C_NOPE, HEAD_G - C_NOPE - C_ROPE)))

    mod = _modulation(cond, ada_w, ada_b)
    n_prompt = BATCH * SEQ
    outs = {k: [] for k in ('rw', 'mlc', 'ckv', 'kpe')}
    for l in range(L):
        modt = mod[l][_MOD_ROW_OF_TILE].reshape(N_TILES, 1, 6 * D_MODEL)
        proj = _in_projection(x, modt, w_proj[l])
        rw, rw_fin = _rwkv(proj, rw_s0[:, l], row2(rw_mu)[l], rw_w0[l][:, None], rw_w2[l], rw_a0[l][:, None],
                           rw_a2[l], rw_g2[l], row2(rw_kk)[l], row2(rw_ka)[l], row2(rw_rk)[l])
        ml, ml_fin = _mlstm(proj, ml_s0[:, l], ml_conv[l], gate_b[l])
        q, ckv, kpe = _mla_pre(proj, row2(mla_q_norm)[l], row2(mla_kv_norm)[l], wuq[l], rope_tabs)
        ckv_all = jnp.concatenate([ckv[:n_prompt]] + [
            t for s in range(DEC_BATCH)
            for t in (cache_mla_ckv[s, l], ckv[n_prompt + s * DEC_SEQ:n_prompt + (s + 1) * DEC_SEQ])], axis=0)
        kpe_all = jnp.concatenate([kpe[:n_prompt]] + [
            t for s in range(DEC_BATCH)
            for t in (kpe_cache[s, l], kpe[n_prompt + s * DEC_SEQ:n_prompt + (s + 1) * DEC_SEQ])], axis=0)
        kf, vf = _mla_kv(ckv_all, kpe_all, wuk[l], wuv[l])
        lk = PAST_LEN + DEC_SEQ
        yc_p = _attention(q[:n_prompt], kf[:n_prompt].reshape(BATCH, SEQ, -1),
                          vf[:n_prompt].reshape(BATCH, SEQ, -1), SEQ)
        yc_s = _attention(q[n_prompt:], kf[n_prompt:].reshape(DEC_BATCH, lk, -1),
                          vf[n_prompt:].reshape(DEC_BATCH, lk, -1), 256)
        yc = jnp.concatenate([yc_p, yc_s], axis=0)
        x1, u2, logits = _merge(x, rw, ml, yc, proj, modt, row2(rw_ln_w)[l], row2(rw_ln_b)[l],
                                row2(ml_ln_w)[l], row2(ml_ln_b)[l], pa[l], pb[l], pc[l], wo[l],
                                row2(ln1_w)[l], row2(ln1_b)[l], rtw[l], rtb[l])
        tok_sorted, w_sorted, inv, items = _route(logits[:, :N_EXPERTS])
        y_sorted = _moe_experts(l, u2[tok_sorted], items, w1s, b1g, b1l, moe_w2, b2, w_sorted[:, None])
        y_slots = y_sorted[inv.reshape(N_TOK, TOP_K).T.reshape(-1)].reshape(TOP_K, N_TOK, D_MODEL)
        x = _final_norm(x1, y_slots, modt, row2(ln2_w)[l], row2(ln2_b)[l])
        outs['rw'].append(jnp.swapaxes(jnp.stack([rw_fin[0][:BATCH], rw_fin[1][:BATCH]], axis=1), -1, -2))
        outs['mlc'].append(ml_fin[:BATCH])
        outs['ckv'].append(ckv[:n_prompt].reshape(BATCH, SEQ, C_KV_LORA))
        outs['kpe'].append(kpe[:n_prompt, C_NOPE:C_NOPE + C_ROPE].reshape(BATCH, SEQ, C_ROPE))

    stack = lambda k: jnp.stack(outs[k], axis=1)
    mlc = stack('mlc')
    return (x[:n_prompt].reshape(BATCH, SEQ, D_MODEL), x[n_prompt:].reshape(DEC_BATCH, DEC_SEQ, D_MODEL),
            stack('rw'), mlc[..., 0:B_DK], mlc[..., N_COL], mlc[..., 0, M_COL], stack('ckv'), stack('kpe'))
```
